```python
import jax, jax.numpy as jnp
from jax import lax
import numpy as np

D_MODEL = 1024
BATCH = 8
SEQ = 2048
DEPTH = 2
DEC_BATCH = 128
DEC_SEQ = 4
PAST_LEN = 16384
PAGE_SIZE = 128

N_MIXERS = 2
N_CONV_LAYERS = (DEPTH + 1) // 2
N_SGU_LAYERS = DEPTH // 2
CONV_WIDTH = 31
D_SGU = D_MODEL
SGU_HEADS = 8
SGU_HEAD_DIM = D_SGU // SGU_HEADS
SGU_CHUNK = 128
D_FF = ((8 * D_MODEL // 3 + 255) // 256) * 256
RMS_EPS = 1e-6
LN_EPS = 1e-5

kernel_name = "conformer_conv_gmlp_hybrid_step"


def _rmsnorm(x, g):
    xf = x.astype(jnp.float32)
    y = xf * lax.rsqrt(jnp.mean(xf * xf, axis=-1, keepdims=True) + RMS_EPS)
    return (y * g.astype(jnp.float32)).astype(x.dtype)


def _layernorm(x, g, b):
    xf = x.astype(jnp.float32)
    mu = jnp.mean(xf, axis=-1, keepdims=True)
    var = jnp.mean(jnp.square(xf - mu), axis=-1, keepdims=True)
    y = (xf - mu) * lax.rsqrt(var + LN_EPS)
    return (y * g.astype(jnp.float32) + b.astype(jnp.float32)).astype(x.dtype)


def _conv_mixer(h, ctx, w_pw1, b_pw1, w_dw, b_dw, ln_g, ln_b, w_pw2, b_pw2):
    a = h @ w_pw1 + b_pw1
    glu = a[..., :D_MODEL] * jax.nn.sigmoid(a[..., D_MODEL:])
    full = jnp.concatenate([ctx.astype(glu.dtype), glu], axis=1)
    conv = lax.conv_general_dilated(
        full, w_dw[:, None, :].astype(full.dtype), window_strides=(1,), padding='VALID',
        dimension_numbers=('NWC', 'WIO', 'NWC'), feature_group_count=D_MODEL) + b_dw
    y = jax.nn.silu(_layernorm(conv, ln_g, ln_b))
    out = y @ w_pw2 + b_pw2
    return out, full[:, -(CONV_WIDTH - 1):]


def _sgu_mixer(h, w_in, b_in, ln_g, ln_b, w_s, b_s, w_out, b_out):
    B, L, _ = h.shape
    z = jax.nn.gelu(h @ w_in + b_in, approximate=False)
    u, v = z[..., :D_SGU], z[..., D_SGU:]
    v = _layernorm(v, ln_g, ln_b)
    c = min(L, SGU_CHUNK)
    n = L // c
    vc = v.reshape(B, n, c, SGU_HEADS, SGU_HEAD_DIM)
    mask = jnp.tril(jnp.ones((c, c), dtype=w_s.dtype))
    w = w_s[:, :c, :c] * mask
    mixed = jnp.einsum('hts,bnshd->bnthd', w, vc) + jnp.transpose(b_s[:, :c])[:, :, None]
    out = (u * mixed.reshape(B, L, D_SGU)) @ w_out + b_out
    return out, v


def _swiglu(h, w_gate, w_up, w_down):
    return (jax.nn.silu(h @ w_gate) * (h @ w_up)) @ w_down


def setup_inputs(seed: int = 0) -> dict:
    key = jax.random.key(seed)
    ks = iter(jax.random.split(key, 32))

    def nrm(shape, scale):
        return jax.random.normal(next(ks), shape, jnp.float32) * scale

    NC, NS = N_CONV_LAYERS, N_SGU_LAYERS
    return {
        "x_prompt": nrm((BATCH, SEQ, D_MODEL), 1.0),
        "x_sample": nrm((DEC_BATCH, DEC_SEQ, D_MODEL), 1.0),
        "state_conv": nrm((NC, DEC_BATCH, CONV_WIDTH - 1, D_MODEL), 0.5),
        "conv_norm_g": 1.0 + nrm((NC, D_MODEL), 0.02),
        "conv_w_pw1": nrm((NC, D_MODEL, 2 * D_MODEL), D_MODEL ** -0.5),
        "conv_b_pw1": nrm((NC, 2 * D_MODEL), 0.02),
        "conv_w_dw": nrm((NC, CONV_WIDTH, D_MODEL), CONV_WIDTH ** -0.5),
        "conv_b_dw": nrm((NC, D_MODEL), 0.02),
        "conv_ln_g": 1.0 + nrm((NC, D_MODEL), 0.02),
        "conv_ln_b": nrm((NC, D_MODEL), 0.02),
        "conv_w_pw2": nrm((NC, D_MODEL, D_MODEL), D_MODEL ** -0.5),
        "conv_b_pw2": nrm((NC, D_MODEL), 0.02),
        "sgu_norm_g": 1.0 + nrm((NS, D_MODEL), 0.02),
        "sgu_w_in": nrm((NS, D_MODEL, 2 * D_SGU), D_MODEL ** -0.5),
        "sgu_b_in": nrm((NS, 2 * D_SGU), 0.02),
        "sgu_ln_g": 1.0 + nrm((NS, D_SGU), 0.02),
        "sgu_ln_b": nrm((NS, D_SGU), 0.02),
        "sgu_w_s": nrm((NS, SGU_HEADS, SGU_CHUNK, SGU_CHUNK), 0.5 * SGU_CHUNK ** -0.5),
        "sgu_b_s": 1.0 + nrm((NS, SGU_HEADS, SGU_CHUNK), 0.01),
        "sgu_w_out": nrm((NS, D_SGU, D_MODEL), D_SGU ** -0.5),
        "sgu_b_out": nrm((NS, D_MODEL), 0.02),
        "ffn_norm_g": 1.0 + nrm((DEPTH, D_MODEL), 0.02),
        "ffn_w_gate": nrm((DEPTH, D_MODEL, D_FF), D_MODEL ** -0.5),
        "ffn_w_up": nrm((DEPTH, D_MODEL, D_FF), D_MODEL ** -0.5),
        "ffn_w_down": nrm((DEPTH, D_FF, D_MODEL), D_FF ** -0.5),
        "final_norm_g": 1.0 + nrm((D_MODEL,), 0.02),
    }


def reference(x_prompt, x_sample, state_conv,
              conv_norm_g, conv_w_pw1, conv_b_pw1, conv_w_dw, conv_b_dw, conv_ln_g, conv_ln_b,
              conv_w_pw2, conv_b_pw2,
              sgu_norm_g, sgu_w_in, sgu_b_in, sgu_ln_g, sgu_ln_b, sgu_w_s, sgu_b_s, sgu_w_out, sgu_b_out,
              ffn_norm_g, ffn_w_gate, ffn_w_up, ffn_w_down, final_norm_g):
    xp, xs = x_prompt, x_sample
    conv_p, conv_s, v_s = [], [], []
    for i in range(DEPTH):
        j = i // N_MIXERS
        if i % N_MIXERS == 0:
            params = (conv_w_pw1[j], conv_b_pw1[j], conv_w_dw[j], conv_b_dw[j],
                      conv_ln_g[j], conv_ln_b[j], conv_w_pw2[j], conv_b_pw2[j])
            ctx_p = jnp.zeros((xp.shape[0], CONV_WIDTH - 1, D_MODEL), xp.dtype)
            op, cp = _conv_mixer(_rmsnorm(xp, conv_norm_g[j]), ctx_p, *params)
            os_, cs = _conv_mixer(_rmsnorm(xs, conv_norm_g[j]), state_conv[j], *params)
            conv_p.append(cp)
            conv_s.append(cs)
        else:
            params = (sgu_w_in[j], sgu_b_in[j], sgu_ln_g[j], sgu_ln_b[j],
                      sgu_w_s[j], sgu_b_s[j], sgu_w_out[j], sgu_b_out[j])
            op, _ = _sgu_mixer(_rmsnorm(xp, sgu_norm_g[j]), *params)
            os_, vs = _sgu_mixer(_rmsnorm(xs, sgu_norm_g[j]), *params)
            v_s.append(vs)
        xp = xp + op
        xs = xs + os_
        xp = xp + _swiglu(_rmsnorm(xp, ffn_norm_g[i]), ffn_w_gate[i], ffn_w_up[i], ffn_w_down[i])
        xs = xs + _swiglu(_rmsnorm(xs, ffn_norm_g[i]), ffn_w_gate[i], ffn_w_up[i], ffn_w_down[i])
    y_prompt = _rmsnorm(xp, final_norm_g)
    y_sample = _rmsnorm(xs, final_norm_g)
    new_conv_prompt = jnp.stack(conv_p)
    new_conv_sample = jnp.stack(conv_s)
    new_sgu_v_sample = jnp.stack(v_s)
    return (y_prompt, y_sample, new_conv_prompt, new_conv_sample, new_sgu_v_sample)
```

```python
import functools

import jax
import jax.numpy as jnp
from jax import lax
from jax.experimental import pallas as pl
from jax.experimental.pallas import tpu as pltpu

D_MODEL = 1024
CONV_WIDTH = 31
CTX = CONV_WIDTH - 1
SGU_HEADS = 8
SGU_HEAD_DIM = D_MODEL // SGU_HEADS
SGU_CHUNK = 128
RMS_EPS = 1e-6
LN_EPS = 1e-5

LANES = 128
SUBLANES = 8
N_LANE_CHUNKS = D_MODEL // LANES
CTX_PAD = 32
VMEM_LIMIT = 56 * 1024 * 1024

F32 = jnp.float32
BF16 = jnp.bfloat16


def _rmsnorm(x, g):
    return x * lax.rsqrt(jnp.mean(x * x, axis=-1, keepdims=True) + RMS_EPS) * g


def _layernorm(x, g, b):
    mu = jnp.mean(x, axis=-1, keepdims=True)
    xc = x - mu
    var = jnp.mean(xc * xc, axis=-1, keepdims=True)
    return xc * lax.rsqrt(var + LN_EPS) * g + b


def _sigmoid(x):
    return 1.0 / (1.0 + jnp.exp(-x))


def _silu(x):
    return x * _sigmoid(x)


def _gelu_exact(x):
    return 0.5 * x * (1.0 + lax.erf(x * (2.0 ** -0.5)))


def _dot(a, b):
    return jnp.dot(a, b, preferred_element_type=F32)


def _const_spec(shape):
    nd = len(shape)
    return pl.BlockSpec(shape, lambda *_: (0,) * nd, pipeline_mode=pl.Buffered(1))


def _params(*sem):
    return pltpu.CompilerParams(dimension_semantics=sem, vmem_limit_bytes=VMEM_LIMIT)


def _depthwise_conv_chunked(fbuf, cbuf, wdw_ref, bdw_ref, n_rows, row_chunk):
    shift0 = CTX_PAD - CTX

    def lane_chunk(j, carry):
        bias = bdw_ref[j]
        for r0 in range(0, n_rows, row_chunk):
            acc = jnp.broadcast_to(bias, (row_chunk, LANES))
            for k in range(CONV_WIDTH):
                w_k = wdw_ref[j, pl.ds(k, 1), :]
                acc = acc + w_k * fbuf[j, pl.ds(r0 + k + shift0, row_chunk), :]
            cbuf[j, pl.ds(r0, row_chunk), :] = acc
        return carry

    lax.fori_loop(0, N_LANE_CHUNKS, lane_chunk, 0)


def _conv_prompt_kernel(x_ref, ng_ref, w1_ref, b1_ref, wdw_ref, bdw_ref, lg_ref, lb_ref,
                        w2_ref, b2_ref, o_ref, ctx_ref, fbuf, cbuf, *, tl):
    l = pl.program_id(1)

    @pl.when(l == 0)
    def _():
        fbuf[:, pl.ds(0, CTX_PAD), :] = jnp.zeros((N_LANE_CHUNKS, CTX_PAD, LANES), F32)

    @pl.when(l > 0)
    def _():
        fbuf[:, pl.ds(0, CTX_PAD), :] = fbuf[:, pl.ds(tl, CTX_PAD), :]

    x = x_ref[...]
    h = _rmsnorm(x, ng_ref[...]).astype(BF16)
    a = _dot(h, w1_ref[...]) + b1_ref[...]
    glu = a[:, :D_MODEL] * _sigmoid(a[:, D_MODEL:])
    for j in range(N_LANE_CHUNKS):
        fbuf[j, pl.ds(CTX_PAD, tl), :] = glu[:, j * LANES:(j + 1) * LANES]

    _depthwise_conv_chunked(fbuf, cbuf, wdw_ref, bdw_ref, tl, 64)

    conv = jnp.concatenate([cbuf[j] for j in range(N_LANE_CHUNKS)], axis=1)
    y = _silu(_layernorm(conv, lg_ref[...], lb_ref[...])).astype(BF16)
    o_ref[...] = x + _dot(y, w2_ref[...]) + b2_ref[...]

    @pl.when(l == pl.num_programs(1) - 1)
    def _():
        for j in range(N_LANE_CHUNKS):
            ctx_ref[0, :, j * LANES:(j + 1) * LANES] = fbuf[j, pl.ds(tl + CTX_PAD - CTX, CTX), :]


def _conv_prompt(x, ng, w1, b1, wdw_c, bdw_c, lg, lb, w2, b2, *, batch, seq, tl):
    n_l = seq // tl
    d = D_MODEL
    return pl.pallas_call(
        functools.partial(_conv_prompt_kernel, tl=tl),
        grid=(batch, n_l),
        in_specs=[
            pl.BlockSpec((tl, d), lambda b, l: (b * n_l + l, 0)),
            _const_spec((1, d)),
            _const_spec((d, 2 * d)),
            _const_spec((1, 2 * d)),
            _const_spec(wdw_c.shape),
            _const_spec(bdw_c.shape),
            _const_spec((1, d)),
            _const_spec((1, d)),
            _const_spec((d, d)),
            _const_spec((1, d)),
        ],
        out_specs=[
            pl.BlockSpec((tl, d), lambda b, l: (b * n_l + l, 0)),
            pl.BlockSpec((1, CTX, d), lambda b, l: (b, 0, 0)),
        ],
        out_shape=[
            jax.ShapeDtypeStruct((batch * seq, d), F32),
            jax.ShapeDtypeStruct((batch, CTX, d), F32),
        ],
        scratch_shapes=[
            pltpu.VMEM((N_LANE_CHUNKS, tl + CTX_PAD, LANES), F32),
            pltpu.VMEM((N_LANE_CHUNKS, tl, LANES), F32),
        ],
        compiler_params=_params("arbitrary", "arbitrary"),
        name="conv_mixer_prompt",
    )(x, ng, w1, b1, wdw_c, bdw_c, lg, lb, w2, b2)


def _conv_sample_kernel(x_ref, st_ref, ng_ref, w1_ref, b1_ref, wdw_ref, bdw_ref, lg_ref, lb_ref,
                        w2_ref, b2_ref, o_ref, nst_ref, gbuf, cbuf, *, n_t, bc):
    rows = n_t * bc
    x = x_ref[...].reshape(rows, D_MODEL)
    h = _rmsnorm(x, ng_ref[...]).astype(BF16)
    a = _dot(h, w1_ref[...]) + b1_ref[...]
    glu = a[:, :D_MODEL] * _sigmoid(a[:, D_MODEL:])
    gbuf[...] = glu.reshape(n_t, bc, D_MODEL)

    nst_ref[pl.ds(0, CTX - n_t)] = st_ref[pl.ds(n_t, CTX - n_t)]
    nst_ref[pl.ds(CTX - n_t, n_t)] = gbuf[...]

    def lane_chunk(j, carry):
        lanes = pl.ds(pl.multiple_of(j * LANES, LANES), LANES)
        bias = bdw_ref[:, lanes]
        accs = [jnp.broadcast_to(bias, (bc, LANES)) for _ in range(n_t)]
        for m in range(CTX + n_t):
            f = st_ref[m, :, lanes] if m < CTX else gbuf[m - CTX, :, lanes]
            for t in range(n_t):
                k = m - t
                if 0 <= k < CONV_WIDTH:
                    accs[t] = accs[t] + wdw_ref[pl.ds(k, 1), lanes] * f
        for t in range(n_t):
            cbuf[t, :, lanes] = accs[t]
        return carry

    lax.fori_loop(0, N_LANE_CHUNKS, lane_chunk, 0)

    conv = cbuf[...].reshape(rows, D_MODEL)
    y = _silu(_layernorm(conv, lg_ref[...], lb_ref[...])).astype(BF16)
    out = x + _dot(y, w2_ref[...]) + b2_ref[...]
    o_ref[...] = out.reshape(n_t, bc, D_MODEL)


def _conv_sample(x_t, st_t, ng, w1, b1, wdw, bdw, lg, lb, w2, b2, *, bc):
    n_t, nb, d = x_t.shape
    return pl.pallas_call(
        functools.partial(_conv_sample_kernel, n_t=n_t, bc=bc),
        grid=(nb // bc,),
        in_specs=[
            pl.BlockSpec((n_t, bc, d), lambda i: (0, i, 0)),
            pl.BlockSpec((CTX, bc, d), lambda i: (0, i, 0)),
            _const_spec((1, d)),
            _const_spec((d, 2 * d)),
            _const_spec((1, 2 * d)),
            _const_spec(wdw.shape),
            _const_spec((1, d)),
            _const_spec((1, d)),
            _const_spec((1, d)),
            _const_spec((d, d)),
            _const_spec((1, d)),
        ],
        out_specs=[
            pl.BlockSpec((n_t, bc, d), lambda i: (0, i, 0)),
            pl.BlockSpec((CTX, bc, d), lambda i: (0, i, 0)),
        ],
        out_shape=[
            jax.ShapeDtypeStruct((n_t, nb, d), F32),
            jax.ShapeDtypeStruct((CTX, nb, d), F32),
        ],
        scratch_shapes=[
            pltpu.VMEM((n_t, bc, d), F32),
            pltpu.VMEM((n_t, bc, d), F32),
        ],
        compiler_params=_params("arbitrary"),
        name="conv_mixer_sample",
    )(x_t, st_t, ng, w1, b1, wdw, bdw, lg, lb, w2, b2)


def _ffn_kernel(x_ref, ng_ref, wg_ref, wu_ref, wd_ref, fg_ref, o_ref, *, final_norm):
    x = x_ref[...]
    h = _rmsnorm(x, ng_ref[...]).astype(BF16)
    gate = _dot(h, wg_ref[...])
    up = _dot(h, wu_ref[...])
    act = (_silu(gate) * up).astype(BF16)
    y = x + _dot(act, wd_ref[...])
    if final_norm:
        y = _rmsnorm(y, fg_ref[...])
    o_ref[...] = y


def _ffn(x, ng, wg, wu, wd, fg, *, tm, final_norm):
    n, d = x.shape
    dff = wg.shape[1]
    return pl.pallas_call(
        functools.partial(_ffn_kernel, final_norm=final_norm),
        grid=(n // tm,),
        in_specs=[
            pl.BlockSpec((tm, d), lambda i: (i, 0)),
            _const_spec((1, d)),
            _const_spec((d, dff)),
            _const_spec((d, dff)),
            _const_spec((dff, d)),
            _const_spec((1, d)),
        ],
        out_specs=pl.BlockSpec((tm, d), lambda i: (i, 0)),
        out_shape=jax.ShapeDtypeStruct((n, d), F32),
        compiler_params=_params("arbitrary"),
        name="swiglu_ffn_final" if final_norm else "swiglu_ffn",
    )(x, ng, wg, wu, wd, fg)


def _sgu_prompt_kernel(x_ref, ng_ref, win_ref, bin_ref, lg_ref, lb_ref, ws_ref, bs_ref,
                       wout_ref, bout_ref, o_ref, *, tl):
    x = x_ref[...]
    h = _rmsnorm(x, ng_ref[...]).astype(BF16)
    z = _gelu_exact(_dot(h, win_ref[...]) + bin_ref[...])
    u = z[:, :D_MODEL]
    v = _layernorm(z[:, D_MODEL:], lg_ref[...], lb_ref[...]).astype(BF16)

    c = SGU_CHUNK
    n_c = tl // c
    row = lax.broadcasted_iota(jnp.int32, (c, c), 0)
    col = lax.broadcasted_iota(jnp.int32, (c, c), 1)
    causal = (col <= row).astype(F32)
    mixed_cols = []
    for hd in range(SGU_HEADS):
        w_h = (ws_ref[hd] * causal).astype(BF16)
        lanes = slice(hd * SGU_HEAD_DIM, (hd + 1) * SGU_HEAD_DIM)
        v_h = jnp.concatenate([v[i * c:(i + 1) * c, lanes] for i in range(n_c)], axis=1)
        m_h = _dot(w_h, v_h)
        mixed_cols.append(
            jnp.concatenate([m_h[:, i * SGU_HEAD_DIM:(i + 1) * SGU_HEAD_DIM] for i in range(n_c)], axis=0))
    mixed = jnp.concatenate(mixed_cols, axis=1)
    bias = jnp.concatenate([bs_ref[...]] * n_c, axis=0)
    gated = (u * (mixed + bias)).astype(BF16)
    o_ref[...] = x + _dot(gated, wout_ref[...]) + bout_ref[...]


def _sgu_prompt(x, ng, win, b_in, lg, lb, ws, bs_full, wout, bout, *, tl):
    n, d = x.shape
    return pl.pallas_call(
        functools.partial(_sgu_prompt_kernel, tl=tl),
        grid=(n // tl,),
        in_specs=[
            pl.BlockSpec((tl, d), lambda i: (i, 0)),
            _const_spec((1, d)),
            _const_spec((d, 2 * d)),
            _const_spec((1, 2 * d)),
            _const_spec((1, d)),
            _const_spec((1, d)),
            _const_spec(ws.shape),
            _const_spec(bs_full.shape),
            _const_spec((d, d)),
            _const_spec((1, d)),
        ],
        out_specs=pl.BlockSpec((tl, d), lambda i: (i, 0)),
        out_shape=jax.ShapeDtypeStruct((n, d), F32),
        compiler_params=_params("arbitrary"),
        name="sgu_mixer_prompt",
    )(x, ng, win, b_in, lg, lb, ws, bs_full, wout, bout)


def _sgu_sample_kernel(x_ref, ng_ref, win_ref, bin_ref, lg_ref, lb_ref, w4_ref, b4_ref,
                       wout_ref, bout_ref, o_ref, v_ref, *, n_t, nb):
    x = x_ref[...]
    h = _rmsnorm(x, ng_ref[...]).astype(BF16)
    z = _gelu_exact(_dot(h, win_ref[...]) + bin_ref[...])
    u = z[:, :D_MODEL]
    v = _layernorm(z[:, D_MODEL:], lg_ref[...], lb_ref[...])
    v_ref[...] = v
    mixed = []
    for t in range(n_t):
        m_t = jnp.broadcast_to(b4_ref[pl.ds(t, 1), :], (nb, D_MODEL))
        for s in range(t + 1):
            m_t = m_t + w4_ref[t, pl.ds(s, 1), :] * v[s * nb:(s + 1) * nb, :]
        mixed.append(m_t)
    gated = (u * jnp.concatenate(mixed, axis=0)).astype(BF16)
    o_ref[...] = x + _dot(gated, wout_ref[...]) + bout_ref[...]


def _sgu_sample(x, ng, win, b_in, lg, lb, w4, b4, wout, bout, *, n_t, nb):
    n, d = x.shape
    whole = lambda shape: pl.BlockSpec(shape, lambda i: (0,) * len(shape))
    return pl.pallas_call(
        functools.partial(_sgu_sample_kernel, n_t=n_t, nb=nb),
        grid=(1,),
        in_specs=[whole(a.shape) for a in (x, ng, win, b_in, lg, lb, w4, b4, wout, bout)],
        out_specs=[whole((n, d)), whole((n, d))],
        out_shape=[jax.ShapeDtypeStruct((n, d), F32), jax.ShapeDtypeStruct((n, d), F32)],
        compiler_params=_params("arbitrary"),
        name="sgu_mixer_sample",
    )(x, ng, win, b_in, lg, lb, w4, b4, wout, bout)


def kernel(x_prompt, x_sample, state_conv, conv_norm_g, conv_w_pw1, conv_b_pw1, conv_w_dw, conv_b_dw, conv_ln_g, conv_ln_b, conv_w_pw2, conv_b_pw2, sgu_norm_g, sgu_w_in, sgu_b_in, sgu_ln_g, sgu_ln_b, sgu_w_s, sgu_b_s, sgu_w_out, sgu_b_out, ffn_norm_g, ffn_w_gate, ffn_w_up, ffn_w_down, final_norm_g):
    batch, seq, d = x_prompt.shape
    nb, n_t, _ = x_sample.shape
    row = lambda a: a.reshape(1, -1)

    xp = x_prompt.reshape(batch * seq, d)
    xs = jnp.transpose(x_sample, (1, 0, 2))
    st = jnp.transpose(state_conv[0], (1, 0, 2))

    w1 = conv_w_pw1[0].astype(BF16)
    w2 = conv_w_pw2[0].astype(BF16)
    wdw = conv_w_dw[0]
    wdw_c = jnp.pad(wdw, ((0, CTX_PAD - CONV_WIDTH), (0, 0))).reshape(CTX_PAD, N_LANE_CHUNKS, LANES)
    wdw_c = jnp.transpose(wdw_c, (1, 0, 2))
    bdw_c = conv_b_dw[0].reshape(N_LANE_CHUNKS, 1, LANES)
    conv_args = (row(conv_norm_g[0]), w1, row(conv_b_pw1[0]))
    conv_tail = (row(conv_ln_g[0]), row(conv_ln_b[0]), w2, row(conv_b_pw2[0]))

    xp, conv_p = _conv_prompt(xp, *conv_args, wdw_c, bdw_c, *conv_tail, batch=batch, seq=seq, tl=512)
    xs, conv_s = _conv_sample(xs, st, *conv_args, wdw, row(conv_b_dw[0]), *conv_tail, bc=32)
    xs = xs.reshape(n_t * nb, d)

    fg = row(final_norm_g)
    ffn_w = [(row(ffn_norm_g[i]), ffn_w_gate[i].astype(BF16), ffn_w_up[i].astype(BF16),
              ffn_w_down[i].astype(BF16)) for i in range(2)]
    xp = _ffn(xp, *ffn_w[0], fg, tm=512, final_norm=False)
    xs = _ffn(xs, *ffn_w[0], fg, tm=512, final_norm=False)

    win = sgu_w_in[0].astype(BF16)
    wout = sgu_w_out[0].astype(BF16)
    sgu_args = (row(sgu_norm_g[0]), win, row(sgu_b_in[0]), row(sgu_ln_g[0]), row(sgu_ln_b[0]))
    bs_full = jnp.repeat(jnp.transpose(sgu_b_s[0]), SGU_HEAD_DIM, axis=1)
    w4 = jnp.repeat(jnp.transpose(sgu_w_s[0][:, :n_t, :n_t], (1, 2, 0)), SGU_HEAD_DIM, axis=2)
    b4 = bs_full[:n_t]

    xp = _sgu_prompt(xp, *sgu_args, sgu_w_s[0], bs_full, wout, row(sgu_b_out[0]), tl=512)
    xs, v_s = _sgu_sample(xs, *sgu_args, w4, b4, wout, row(sgu_b_out[0]), n_t=n_t, nb=nb)

    yp = _ffn(xp, *ffn_w[1], fg, tm=512, final_norm=True)
    ys = _ffn(xs, *ffn_w[1], fg, tm=512, final_norm=True)

    y_prompt = yp.reshape(batch, seq, d)
    y_sample = jnp.transpose(ys.reshape(n_t, nb, d), (1, 0, 2))
    new_conv_prompt = conv_p[None]
    new_conv_sample = jnp.transpose(conv_s, (1, 0, 2))[None]
    new_sgu_v_sample = jnp.transpose(v_s.reshape(n_t, nb, d), (1, 0, 2))[None]
    return (y_prompt, y_sample, new_conv_prompt, new_conv_sample, new_sgu_v_sample)
```

```python
import functools

import jax
import jax.numpy as jnp
from jax import lax
from jax.experimental import pallas as pl
from jax.experimental.pallas import tpu as pltpu

D_MODEL = 1024
CONV_WIDTH = 31
CTX = CONV_WIDTH - 1
SGU_HEADS = 8
SGU_HEAD_DIM = D_MODEL // SGU_HEADS
SGU_CHUNK = 128
RMS_EPS = 1e-6
LN_EPS = 1e-5

LANES = 128
BF16_SUBLANES = 16
N_LANE_CHUNKS = D_MODEL // LANES
CTX_PAD = 32
VMEM_LIMIT = 56 * 1024 * 1024
TOKEN_TILE = 512
CAST_STEPS = 32

F32 = jnp.float32
BF16 = jnp.bfloat16


def _rmsnorm(x, g):
    return x * lax.rsqrt(jnp.mean(x * x, axis=-1, keepdims=True) + RMS_EPS) * g


def _layernorm(x, g, b):
    mu = jnp.mean(x, axis=-1, keepdims=True)
    xc = x - mu
    var = jnp.mean(xc * xc, axis=-1, keepdims=True)
    return xc * lax.rsqrt(var + LN_EPS) * g + b


def _sigmoid(x):
    return 1.0 / (1.0 + jnp.exp(-x))


def _silu(x):
    return x * _sigmoid(x)


def _gelu_exact(x):
    return 0.5 * x * (1.0 + lax.erf(x * (2.0 ** -0.5)))


def _dot(a, b):
    return jnp.dot(a, b, preferred_element_type=F32)


def _const_spec(shape):
    nd = len(shape)
    return pl.BlockSpec(shape, lambda *_: (0,) * nd, pipeline_mode=pl.Buffered(1))


def _params(*sem):
    return pltpu.CompilerParams(dimension_semantics=sem, vmem_limit_bytes=VMEM_LIMIT)


def _cast_item(stacked, layer):
    rows = stacked.shape[1]
    steps = CAST_STEPS
    while rows % (steps * BF16_SUBLANES):
        steps //= 2
    return stacked, layer, steps


def _cast_specs(items, step_of):
    in_specs, out_specs, shapes = [], [], []
    for stacked, layer, steps in items:
        _, rows, cols = stacked.shape
        slab = rows // steps
        in_specs.append(pl.BlockSpec(
            (None, slab, cols),
            lambda *g, layer=layer, steps=steps: (layer, jnp.minimum(step_of(*g), steps - 1), 0)))
        out_specs.append(pl.BlockSpec(
            (slab, cols), lambda *g, steps=steps: (jnp.minimum(step_of(*g), steps - 1), 0)))
        shapes.append(jax.ShapeDtypeStruct((rows, cols), BF16))
    return in_specs, out_specs, shapes


def _cast_slabs(src_refs, dst_refs):
    for s, d in zip(src_refs, dst_refs):
        d[...] = s[...].astype(BF16)


def _conv_prompt_kernel(x_ref, ng_ref, w1_ref, b1_ref, wdw_ref, bdw_ref, lg_ref, lb_ref,
                        w2_ref, b2_ref, *rest, tl, n_cast):
    cast_src = rest[:n_cast]
    o_ref, ctx_ref = rest[n_cast:n_cast + 2]
    cast_dst = rest[n_cast + 2:2 * n_cast + 2]
    fbuf, cbuf = rest[2 * n_cast + 2:]
    _cast_slabs(cast_src, cast_dst)
    l = pl.program_id(1)

    @pl.when(l == 0)
    def _():
        fbuf[:, pl.ds(0, CTX_PAD), :] = jnp.zeros((N_LANE_CHUNKS, CTX_PAD, LANES), F32)

    @pl.when(l > 0)
    def _():
        fbuf[:, pl.ds(0, CTX_PAD), :] = fbuf[:, pl.ds(tl, CTX_PAD), :]

    x = x_ref[...]
    h = _rmsnorm(x, ng_ref[...]).astype(BF16)
    a = _dot(h, w1_ref[...]) + b1_ref[...]
    glu = a[:, :D_MODEL] * _sigmoid(a[:, D_MODEL:])
    for j in range(N_LANE_CHUNKS):
        fbuf[j, pl.ds(CTX_PAD, tl), :] = glu[:, j * LANES:(j + 1) * LANES]

    shift0 = CTX_PAD - CTX
    row_chunk = 64

    def lane_chunk(j, carry):
        bias = bdw_ref[j]
        for r0 in range(0, tl, row_chunk):
            acc = jnp.broadcast_to(bias, (row_chunk, LANES))
            for k in range(CONV_WIDTH):
                w_k = wdw_ref[j, pl.ds(k, 1), :]
                acc = acc + w_k * fbuf[j, pl.ds(r0 + k + shift0, row_chunk), :]
            cbuf[j, pl.ds(r0, row_chunk), :] = acc
        return carry

    lax.fori_loop(0, N_LANE_CHUNKS, lane_chunk, 0)

    conv = jnp.concatenate([cbuf[j] for j in range(N_LANE_CHUNKS)], axis=1)
    y = _silu(_layernorm(conv, lg_ref[...], lb_ref[...])).astype(BF16)
    o_ref[...] = x + _dot(y, w2_ref[...]) + b2_ref[...]

    @pl.when(l == pl.num_programs(1) - 1)
    def _():
        for j in range(N_LANE_CHUNKS):
            ctx_ref[0, :, j * LANES:(j + 1) * LANES] = fbuf[j, pl.ds(tl + CTX_PAD - CTX, CTX), :]


def _conv_prompt(x, ng, w1, b1, wdw_c, bdw_c, lg, lb, w2, b2, cast_items, *, batch, seq, tl):
    n_l = seq // tl
    d = D_MODEL
    assert batch * n_l >= CAST_STEPS
    cast_in, cast_out, cast_shapes = _cast_specs(cast_items, lambda b, l: b * n_l + l)
    outs = pl.pallas_call(
        functools.partial(_conv_prompt_kernel, tl=tl, n_cast=len(cast_items)),
        grid=(batch, n_l),
        in_specs=[
            pl.BlockSpec((tl, d), lambda b, l: (b * n_l + l, 0)),
            _const_spec((1, d)),
            _const_spec((d, 2 * d)),
            _const_spec((1, 2 * d)),
            _const_spec(wdw_c.shape),
            _const_spec(bdw_c.shape),
            _const_spec((1, d)),
            _const_spec((1, d)),
            _const_spec((d, d)),
            _const_spec((1, d)),
        ] + cast_in,
        out_specs=[
            pl.BlockSpec((tl, d), lambda b, l: (b * n_l + l, 0)),
            pl.BlockSpec((1, CTX, d), lambda b, l: (b, 0, 0)),
        ] + cast_out,
        out_shape=[
            jax.ShapeDtypeStruct((batch * seq, d), F32),
            jax.ShapeDtypeStruct((batch, CTX, d), F32),
        ] + cast_shapes,
        scratch_shapes=[
            pltpu.VMEM((N_LANE_CHUNKS, tl + CTX_PAD, LANES), F32),
            pltpu.VMEM((N_LANE_CHUNKS, tl, LANES), F32),
        ],
        compiler_params=_params("arbitrary", "arbitrary"),
        name="conv_mixer_prompt",
    )(x, ng, w1, b1, wdw_c, bdw_c, lg, lb, w2, b2, *[item[0] for item in cast_items])
    return outs[0], outs[1], outs[2:]


def _conv_sample_kernel(x_ref, st_ref, ng_ref, w1_ref, b1_ref, wdw_ref, bdw_ref, lg_ref, lb_ref,
                        w2_ref, b2_ref, o_ref, nst_ref, gbuf, cbuf, *, n_t, bc):
    x = jnp.concatenate([x_ref[:, t, :] for t in range(n_t)], axis=0)
    h = _rmsnorm(x, ng_ref[...]).astype(BF16)
    a = _dot(h, w1_ref[...]) + b1_ref[...]
    glu = a[:, :D_MODEL] * _sigmoid(a[:, D_MODEL:])
    for t in range(n_t):
        gbuf[t] = glu[t * bc:(t + 1) * bc, :]

    for m in range(CTX - n_t):
        nst_ref[:, m, :] = st_ref[:, m + n_t, :]
    for t in range(n_t):
        nst_ref[:, CTX - n_t + t, :] = gbuf[t]

    def lane_chunk(j, carry):
        lanes = pl.ds(pl.multiple_of(j * LANES, LANES), LANES)
        bias = bdw_ref[:, lanes]
        accs = [jnp.broadcast_to(bias, (bc, LANES)) for _ in range(n_t)]
        for m in range(CTX + n_t):
            f = st_ref[:, m, lanes] if m < CTX else gbuf[m - CTX, :, lanes]
            for t in range(n_t):
                k = m - t
                if 0 <= k < CONV_WIDTH:
                    accs[t] = accs[t] + wdw_ref[pl.ds(k, 1), lanes] * f
        for t in range(n_t):
            cbuf[t, :, lanes] = accs[t]
        return carry

    lax.fori_loop(0, N_LANE_CHUNKS, lane_chunk, 0)

    conv = jnp.concatenate([cbuf[t] for t in range(n_t)], axis=0)
    y = _silu(_layernorm(conv, lg_ref[...], lb_ref[...])).astype(BF16)
    out = x + _dot(y, w2_ref[...]) + b2_ref[...]
    for t in range(n_t):
        o_ref[:, t, :] = out[t * bc:(t + 1) * bc, :]


def _conv_sample(x, st, ng, w1, b1, wdw, bdw, lg, lb, w2, b2, *, bc):
    nb, n_t, d = x.shape
    return pl.pallas_call(
        functools.partial(_conv_sample_kernel, n_t=n_t, bc=bc),
        grid=(nb // bc,),
        in_specs=[
            pl.BlockSpec((bc, n_t, d), lambda i: (i, 0, 0)),
            pl.BlockSpec((bc, CTX, d), lambda i: (i, 0, 0)),
            _const_spec((1, d)),
            _const_spec((d, 2 * d)),
            _const_spec((1, 2 * d)),
            _const_spec(wdw.shape),
            _const_spec((1, d)),
            _const_spec((1, d)),
            _const_spec((1, d)),
            _const_spec((d, d)),
            _const_spec((1, d)),
        ],
        out_specs=[
            pl.BlockSpec((bc, n_t, d), lambda i: (i, 0, 0)),
            pl.BlockSpec((bc, CTX, d), lambda i: (i, 0, 0)),
        ],
        out_shape=[
            jax.ShapeDtypeStruct((nb, n_t, d), F32),
            jax.ShapeDtypeStruct((nb, CTX, d), F32),
        ],
        scratch_shapes=[
            pltpu.VMEM((n_t, bc, d), F32),
            pltpu.VMEM((n_t, bc, d), F32),
        ],
        compiler_params=_params("arbitrary"),
        name="conv_mixer_sample",
    )(x, st, ng, w1, b1, wdw, bdw, lg, lb, w2, b2)


def _ffn_kernel(xp_ref, xs_ref, ng_ref, wg_ref, wu_ref, wd_ref, fg_ref, *rest,
                final_norm, n_prompt_tiles, n_cast):
    cast_src = rest[:n_cast]
    op_ref, os_ref = rest[n_cast:n_cast + 2]
    cast_dst = rest[n_cast + 2:]
    _cast_slabs(cast_src, cast_dst)

    def ffn(x):
        h = _rmsnorm(x, ng_ref[...]).astype(BF16)
        gate = _dot(h, wg_ref[...])
        up = _dot(h, wu_ref[...])
        act = (_silu(gate) * up).astype(BF16)
        y = x + _dot(act, wd_ref[...])
        if final_norm:
            y = _rmsnorm(y, fg_ref[...])
        return y

    i = pl.program_id(0)

    @pl.when(i < n_prompt_tiles)
    def _():
        op_ref[...] = ffn(xp_ref[...])

    @pl.when(i == n_prompt_tiles)
    def _():
        nb, n_t, _ = xs_ref.shape
        y = ffn(jnp.concatenate([xs_ref[:, t, :] for t in range(n_t)], axis=0))
        for t in range(n_t):
            os_ref[:, t, :] = y[t * nb:(t + 1) * nb, :]


def _ffn(xp, xs, ng, wg, wu, wd, fg, cast_items, *, tm, final_norm):
    n, d = xp.shape
    dff = wg.shape[1]
    n_tiles = n // tm
    assert n_tiles >= CAST_STEPS
    cast_in, cast_out, cast_shapes = _cast_specs(cast_items, lambda i: i)
    prompt_spec = pl.BlockSpec((tm, d), lambda i: (jnp.minimum(i, n_tiles - 1), 0))
    outs = pl.pallas_call(
        functools.partial(_ffn_kernel, final_norm=final_norm, n_prompt_tiles=n_tiles,
                          n_cast=len(cast_items)),
        grid=(n_tiles + 1,),
        in_specs=[
            prompt_spec,
            _const_spec(xs.shape),
            _const_spec((1, d)),
            _const_spec((d, dff)),
            _const_spec((d, dff)),
            _const_spec((dff, d)),
            _const_spec((1, d)),
        ] + cast_in,
        out_specs=[prompt_spec, pl.BlockSpec(xs.shape, lambda i: (0, 0, 0))] + cast_out,
        out_shape=[jax.ShapeDtypeStruct((n, d), F32), jax.ShapeDtypeStruct(xs.shape, F32)] + cast_shapes,
        compiler_params=_params("arbitrary"),
        name="swiglu_ffn_final" if final_norm else "swiglu_ffn",
    )(xp, xs, ng, wg, wu, wd, fg, *[item[0] for item in cast_items])
    return outs[0], outs[1], outs[2:]


def _sgu_prompt_kernel(x_ref, ng_ref, win_ref, bin_ref, lg_ref, lb_ref, ws_ref, bs_ref,
                       wout_ref, bout_ref, o_ref, *, tl):
    x = x_ref[...]
    h = _rmsnorm(x, ng_ref[...]).astype(BF16)
    z = _gelu_exact(_dot(h, win_ref[...]) + bin_ref[...])
    u = z[:, :D_MODEL]
    v = _layernorm(z[:, D_MODEL:], lg_ref[...], lb_ref[...]).astype(BF16)

    c = SGU_CHUNK
    n_c = tl // c
    row = lax.broadcasted_iota(jnp.int32, (c, c), 0)
    col = lax.broadcasted_iota(jnp.int32, (c, c), 1)
    causal = (col <= row).astype(F32)
    mixed_cols = []
    for hd in range(SGU_HEADS):
        w_h = (ws_ref[hd] * causal).astype(BF16)
        lanes = slice(hd * SGU_HEAD_DIM, (hd + 1) * SGU_HEAD_DIM)
        v_h = jnp.concatenate([v[i * c:(i + 1) * c, lanes] for i in range(n_c)], axis=1)
        m_h = _dot(w_h, v_h)
        mixed_cols.append(
            jnp.concatenate([m_h[:, i * SGU_HEAD_DIM:(i + 1) * SGU_HEAD_DIM] for i in range(n_c)], axis=0))
    mixed = jnp.concatenate(mixed_cols, axis=1)
    bias = jnp.concatenate([bs_ref[...]] * n_c, axis=0)
    gated = (u * (mixed + bias)).astype(BF16)
    o_ref[...] = x + _dot(gated, wout_ref[...]) + bout_ref[...]


def _sgu_prompt(x, ng, win, b_in, lg, lb, ws, bs_full, wout, bout, *, tl):
    n, d = x.shape
    return pl.pallas_call(
        functools.partial(_sgu_prompt_kernel, tl=tl),
        grid=(n // tl,),
        in_specs=[
            pl.BlockSpec((tl, d), lambda i: (i, 0)),
            _const_spec((1, d)),
            _const_spec((d, 2 * d)),
            _const_spec((1, 2 * d)),
            _const_spec((1, d)),
            _const_spec((1, d)),
            _const_spec(ws.shape),
            _const_spec(bs_full.shape),
            _const_spec((d, d)),
            _const_spec((1, d)),
        ],
        out_specs=pl.BlockSpec((tl, d), lambda i: (i, 0)),
        out_shape=jax.ShapeDtypeStruct((n, d), F32),
        compiler_params=_params("arbitrary"),
        name="sgu_mixer_prompt",
    )(x, ng, win, b_in, lg, lb, ws, bs_full, wout, bout)


def _sgu_sample_kernel(x_ref, ng_ref, win_ref, bin_ref, lg_ref, lb_ref, w4_ref, b4_ref,
                       wout_ref, bout_ref, o_ref, v_ref, *, n_t, nb):
    x = jnp.concatenate([x_ref[:, t, :] for t in range(n_t)], axis=0)
    h = _rmsnorm(x, ng_ref[...]).astype(BF16)
    z = _gelu_exact(_dot(h, win_ref[...]) + bin_ref[...])
    u = z[:, :D_MODEL]
    v = _layernorm(z[:, D_MODEL:], lg_ref[...], lb_ref[...])
    mixed = []
    for t in range(n_t):
        v_ref[:, t, :] = v[t * nb:(t + 1) * nb, :]
        m_t = jnp.broadcast_to(b4_ref[pl.ds(t, 1), :], (nb, D_MODEL))
        for s in range(t + 1):
            m_t = m_t + w4_ref[t, pl.ds(s, 1), :] * v[s * nb:(s + 1) * nb, :]
        mixed.append(m_t)
    gated = (u * jnp.concatenate(mixed, axis=0)).astype(BF16)
    out = x + _dot(gated, wout_ref[...]) + bout_ref[...]
    for t in range(n_t):
        o_ref[:, t, :] = out[t * nb:(t + 1) * nb, :]


def _sgu_sample(x, ng, win, b_in, lg, lb, w4, b4, wout, bout):
    nb, n_t, d = x.shape
    whole = lambda shape: pl.BlockSpec(shape, lambda i: (0,) * len(shape))
    return pl.pallas_call(
        functools.partial(_sgu_sample_kernel, n_t=n_t, nb=nb),
        grid=(1,),
        in_specs=[whole(a.shape) for a in (x, ng, win, b_in, lg, lb, w4, b4, wout, bout)],
        out_specs=[whole(x.shape), whole(x.shape)],
        out_shape=[jax.ShapeDtypeStruct(x.shape, F32), jax.ShapeDtypeStruct(x.shape, F32)],
        compiler_params=_params("arbitrary"),
        name="sgu_mixer_sample",
    )(x, ng, win, b_in, lg, lb, w4, b4, wout, bout)


def kernel(x_prompt, x_sample, state_conv, conv_norm_g, conv_w_pw1, conv_b_pw1, conv_w_dw, conv_b_dw, conv_ln_g, conv_ln_b, conv_w_pw2, conv_b_pw2, sgu_norm_g, sgu_w_in, sgu_b_in, sgu_ln_g, sgu_ln_b, sgu_w_s, sgu_b_s, sgu_w_out, sgu_b_out, ffn_norm_g, ffn_w_gate, ffn_w_up, ffn_w_down, final_norm_g):
    batch, seq, d = x_prompt.shape
    nb, n_t, _ = x_sample.shape
    row = lambda a: a.reshape(1, -1)
    ffn_items = lambda i: [_cast_item(w, i) for w in (ffn_w_gate, ffn_w_up, ffn_w_down)]

    xp = x_prompt.reshape(batch * seq, d)

    w1 = conv_w_pw1[0].astype(BF16)
    w2 = conv_w_pw2[0].astype(BF16)
    wdw = conv_w_dw[0]
    wdw_c = jnp.pad(wdw, ((0, CTX_PAD - CONV_WIDTH), (0, 0))).reshape(CTX_PAD, N_LANE_CHUNKS, LANES)
    wdw_c = jnp.transpose(wdw_c, (1, 0, 2))
    bdw_c = conv_b_dw[0].reshape(N_LANE_CHUNKS, 1, LANES)
    conv_args = (row(conv_norm_g[0]), w1, row(conv_b_pw1[0]))
    conv_tail = (row(conv_ln_g[0]), row(conv_ln_b[0]), w2, row(conv_b_pw2[0]))

    xp, conv_p, ffn0_bf16 = _conv_prompt(xp, *conv_args, wdw_c, bdw_c, *conv_tail, ffn_items(0),
                                         batch=batch, seq=seq, tl=TOKEN_TILE)
    xs, conv_s = _conv_sample(x_sample, state_conv[0], *conv_args, wdw, row(conv_b_dw[0]), *conv_tail, bc=32)

    fg = row(final_norm_g)
    later_items = [_cast_item(sgu_w_in, 0), _cast_item(sgu_w_out, 0)] + ffn_items(1)
    xp, xs, later_bf16 = _ffn(xp, xs, row(ffn_norm_g[0]), *ffn0_bf16, fg, later_items,
                              tm=TOKEN_TILE, final_norm=False)

    win, wout = later_bf16[:2]
    sgu_args = (row(sgu_norm_g[0]), win, row(sgu_b_in[0]), row(sgu_ln_g[0]), row(sgu_ln_b[0]))
    bs_full = jnp.repeat(jnp.transpose(sgu_b_s[0]), SGU_HEAD_DIM, axis=1)
    w4 = jnp.repeat(jnp.transpose(sgu_w_s[0][:, :n_t, :n_t], (1, 2, 0)), SGU_HEAD_DIM, axis=2)
    b4 = bs_full[:n_t]

    xp = _sgu_prompt(xp, *sgu_args, sgu_w_s[0], bs_full, wout, row(sgu_b_out[0]), tl=TOKEN_TILE)
    xs, v_s = _sgu_sample(xs, *sgu_args, w4, b4, wout, row(sgu_b_out[0]))

    yp, ys, _ = _ffn(xp, xs, row(ffn_norm_g[1]), *later_bf16[2:], fg, [], tm=TOKEN_TILE, final_norm=True)

    return (yp.reshape(batch, seq, d), ys, conv_p[None], conv_s[None], v_s[None])
```

```python
import functools

import jax
import jax.numpy as jnp
from jax import lax
from jax.experimental import pallas as pl
from jax.experimental.pallas import tpu as pltpu

D_MODEL = 1024
CONV_WIDTH = 31
CTX = CONV_WIDTH - 1
SGU_HEADS = 8
SGU_HEAD_DIM = D_MODEL // SGU_HEADS
SGU_CHUNK = 128
RMS_EPS = 1e-6
LN_EPS = 1e-5

LANES = 128
BF16_SUBLANES = 16
N_LANE_CHUNKS = D_MODEL // LANES
CTX_PAD = 32
VMEM_LIMIT = 56 * 1024 * 1024
TOKEN_TILE = 512
FFN_ROW_GROUPS = 2
CAST_SLABS = 8
CAST_EVERY = 4

F32 = jnp.float32
BF16 = jnp.bfloat16


def _rmsnorm(x, g):
    return x * lax.rsqrt(jnp.mean(x * x, axis=-1, keepdims=True) + RMS_EPS) * g


def _layernorm(x, g, b):
    mu = jnp.mean(x, axis=-1, keepdims=True)
    xc = x - mu
    var = jnp.mean(xc * xc, axis=-1, keepdims=True)
    return xc * lax.rsqrt(var + LN_EPS) * g + b


def _sigmoid(x):
    return 1.0 / (1.0 + jnp.exp(-x))


def _silu(x):
    return x * _sigmoid(x)


def _gelu_exact(x):
    return 0.5 * x * (1.0 + lax.erf(x * (2.0 ** -0.5)))


def _dot(a, b):
    return jnp.dot(a, b, preferred_element_type=F32)


def _const_spec(shape):
    nd = len(shape)
    return pl.BlockSpec(shape, lambda *_: (0,) * nd, pipeline_mode=pl.Buffered(1))


def _params(*sem):
    return pltpu.CompilerParams(dimension_semantics=sem, vmem_limit_bytes=VMEM_LIMIT)


def _row_groups(n_rows, n_groups=1):
    size = n_rows // n_groups
    return [pl.ds(g * size, size) for g in range(n_groups)]


def _time_major(x_ref):
    return jnp.concatenate([x_ref[:, t, :] for t in range(x_ref.shape[1])], axis=0)


def _store_time_major(o_ref, y):
    nb, n_t, _ = o_ref.shape
    for t in range(n_t):
        o_ref[:, t, :] = y[t * nb:(t + 1) * nb, :]


def _cast_specs(items):
    slab_of = lambda s: jnp.minimum(s // CAST_EVERY, CAST_SLABS - 1)
    in_specs, out_specs, shapes = [], [], []
    for stacked, layer in items:
        _, rows, cols = stacked.shape
        assert rows % (CAST_SLABS * BF16_SUBLANES) == 0
        slab = rows // CAST_SLABS
        in_specs.append(pl.BlockSpec((None, slab, cols), lambda s, layer=layer: (layer, slab_of(s), 0)))
        out_specs.append(pl.BlockSpec((slab, cols), lambda s: (slab_of(s), 0)))
        shapes.append(jax.ShapeDtypeStruct((rows, cols), BF16))
    return in_specs, out_specs, shapes


def _cast_slabs(src_refs, dst_refs):
    if not src_refs:
        return

    @pl.when(pl.program_id(0) % CAST_EVERY == 0)
    def _():
        for s, d in zip(src_refs, dst_refs):
            d[...] = s[...].astype(BF16)


def _conv_prompt_kernel(x_ref, ng_ref, w1_ref, b1_ref, wdw_ref, bdw_ref, lg_ref, lb_ref,
                        w2_ref, b2_ref, *rest, tl, tiles_per_seq, n_cast):
    cast_src = rest[:n_cast]
    o_ref, ctx_ref = rest[n_cast:n_cast + 2]
    cast_dst = rest[n_cast + 2:2 * n_cast + 2]
    fbuf, cbuf = rest[2 * n_cast + 2:]
    _cast_slabs(cast_src, cast_dst)
    l = pl.program_id(0) % tiles_per_seq

    @pl.when(l == 0)
    def _():
        fbuf[:, pl.ds(0, CTX_PAD), :] = jnp.zeros((N_LANE_CHUNKS, CTX_PAD, LANES), F32)

    @pl.when(l > 0)
    def _():
        fbuf[:, pl.ds(0, CTX_PAD), :] = fbuf[:, pl.ds(tl, CTX_PAD), :]

    for rows in _row_groups(tl):
        h = _rmsnorm(x_ref[rows, :], ng_ref[...]).astype(BF16)
        a = _dot(h, w1_ref[...]) + b1_ref[...]
        glu = a[:, :D_MODEL] * _sigmoid(a[:, D_MODEL:])
        for j in range(N_LANE_CHUNKS):
            fbuf[j, pl.ds(CTX_PAD + rows.start, rows.size), :] = glu[:, j * LANES:(j + 1) * LANES]

    shift0 = CTX_PAD - CTX
    row_chunk = 64

    def lane_chunk(j, carry):
        bias = bdw_ref[j]
        for r0 in range(0, tl, row_chunk):
            acc = jnp.broadcast_to(bias, (row_chunk, LANES))
            for k in range(CONV_WIDTH):
                w_k = wdw_ref[j, pl.ds(k, 1), :]
                acc = acc + w_k * fbuf[j, pl.ds(r0 + k + shift0, row_chunk), :]
            cbuf[j, pl.ds(r0, row_chunk), :] = acc
        return carry

    lax.fori_loop(0, N_LANE_CHUNKS, lane_chunk, 0)

    for rows in _row_groups(tl):
        conv = jnp.concatenate([cbuf[j, rows, :] for j in range(N_LANE_CHUNKS)], axis=1)
        y = _silu(_layernorm(conv, lg_ref[...], lb_ref[...])).astype(BF16)
        o_ref[rows, :] = x_ref[rows, :] + _dot(y, w2_ref[...]) + b2_ref[...]

    @pl.when(l == tiles_per_seq - 1)
    def _():
        for j in range(N_LANE_CHUNKS):
            ctx_ref[0, :, j * LANES:(j + 1) * LANES] = fbuf[j, pl.ds(tl + CTX_PAD - CTX, CTX), :]


def _conv_prompt(x, ng, w1, b1, wdw_c, bdw_c, lg, lb, w2, b2, cast_items, *, batch, seq, tl):
    tiles_per_seq = seq // tl
    d = D_MODEL
    cast_in, cast_out, cast_shapes = _cast_specs(cast_items)
    outs = pl.pallas_call(
        functools.partial(_conv_prompt_kernel, tl=tl, tiles_per_seq=tiles_per_seq, n_cast=len(cast_items)),
        grid=(batch * tiles_per_seq,),
        in_specs=[
            pl.BlockSpec((tl, d), lambda s: (s, 0)),
            _const_spec((1, d)),
            _const_spec((d, 2 * d)),
            _const_spec((1, 2 * d)),
            _const_spec(wdw_c.shape),
            _const_spec(bdw_c.shape),
            _const_spec((1, d)),
            _const_spec((1, d)),
            _const_spec((d, d)),
            _const_spec((1, d)),
        ] + cast_in,
        out_specs=[
            pl.BlockSpec((tl, d), lambda s: (s, 0)),
            pl.BlockSpec((1, CTX, d), lambda s: (s // tiles_per_seq, 0, 0)),
        ] + cast_out,
        out_shape=[
            jax.ShapeDtypeStruct((batch * seq, d), F32),
            jax.ShapeDtypeStruct((batch, CTX, d), F32),
        ] + cast_shapes,
        scratch_shapes=[
            pltpu.VMEM((N_LANE_CHUNKS, tl + CTX_PAD, LANES), F32),
            pltpu.VMEM((N_LANE_CHUNKS, tl, LANES), F32),
        ],
        compiler_params=_params("arbitrary"),
        name="conv_mixer_prompt",
    )(x, ng, w1, b1, wdw_c, bdw_c, lg, lb, w2, b2, *[item[0] for item in cast_items])
    return outs[0], outs[1], outs[2:]


def _conv_sample_kernel(x_ref, st_ref, ng_ref, w1_ref, b1_ref, wdw_ref, bdw_ref, lg_ref, lb_ref,
                        w2_ref, b2_ref, o_ref, nst_ref, gbuf, cbuf, *, n_t, bc):
    x = _time_major(x_ref)
    h = _rmsnorm(x, ng_ref[...]).astype(BF16)
    a = _dot(h, w1_ref[...]) + b1_ref[...]
    glu = a[:, :D_MODEL] * _sigmoid(a[:, D_MODEL:])
    for t in range(n_t):
        gbuf[t] = glu[t * bc:(t + 1) * bc, :]

    nst_ref[pl.ds(0, CTX - n_t)] = st_ref[pl.ds(n_t, CTX - n_t)]
    nst_ref[pl.ds(CTX - n_t, n_t)] = gbuf[...]

    def lane_chunk(j, carry):
        lanes = pl.ds(pl.multiple_of(j * LANES, LANES), LANES)
        bias = bdw_ref[:, lanes]
        accs = [jnp.broadcast_to(bias, (bc, LANES)) for _ in range(n_t)]
        for m in range(CTX + n_t):
            f = st_ref[m, :, lanes] if m < CTX else gbuf[m - CTX, :, lanes]
            for t in range(n_t):
                k = m - t
                if 0 <= k < CONV_WIDTH:
                    accs[t] = accs[t] + wdw_ref[pl.ds(k, 1), lanes] * f
        for t in range(n_t):
            cbuf[t, :, lanes] = accs[t]
        return carry

    lax.fori_loop(0, N_LANE_CHUNKS, lane_chunk, 0)

    conv = jnp.concatenate([cbuf[t] for t in range(n_t)], axis=0)
    y = _silu(_layernorm(conv, lg_ref[...], lb_ref[...])).astype(BF16)
    _store_time_major(o_ref, x + _dot(y, w2_ref[...]) + b2_ref[...])


def _conv_sample(x, st_t, ng, w1, b1, wdw, bdw, lg, lb, w2, b2, *, bc):
    nb, n_t, d = x.shape
    return pl.pallas_call(
        functools.partial(_conv_sample_kernel, n_t=n_t, bc=bc),
        grid=(nb // bc,),
        in_specs=[
            pl.BlockSpec((bc, n_t, d), lambda i: (i, 0, 0)),
            pl.BlockSpec((CTX, bc, d), lambda i: (0, i, 0)),
            _const_spec((1, d)),
            _const_spec((d, 2 * d)),
            _const_spec((1, 2 * d)),
            _const_spec(wdw.shape),
            _const_spec((1, d)),
            _const_spec((1, d)),
            _const_spec((1, d)),
            _const_spec((d, d)),
            _const_spec((1, d)),
        ],
        out_specs=[
            pl.BlockSpec((bc, n_t, d), lambda i: (i, 0, 0)),
            pl.BlockSpec((CTX, bc, d), lambda i: (0, i, 0)),
        ],
        out_shape=[
            jax.ShapeDtypeStruct((nb, n_t, d), F32),
            jax.ShapeDtypeStruct((CTX, nb, d), F32),
        ],
        scratch_shapes=[
            pltpu.VMEM((n_t, bc, d), F32),
            pltpu.VMEM((n_t, bc, d), F32),
        ],
        compiler_params=_params("arbitrary"),
        name="conv_mixer_sample",
    )(x, st_t, ng, w1, b1, wdw, bdw, lg, lb, w2, b2)


def _ffn_kernel(xp_ref, xs_ref, ng_ref, wg_ref, wu_ref, wd_ref, fg_ref, *rest,
                final_norm, n_prompt_tiles, n_cast):
    cast_src = rest[:n_cast]
    op_ref, os_ref = rest[n_cast:n_cast + 2]
    cast_dst = rest[n_cast + 2:]
    _cast_slabs(cast_src, cast_dst)

    def ffn(x):
        h = _rmsnorm(x, ng_ref[...]).astype(BF16)
        gate = _dot(h, wg_ref[...])
        up = _dot(h, wu_ref[...])
        act = (_silu(gate) * up).astype(BF16)
        y = x + _dot(act, wd_ref[...])
        if final_norm:
            y = _rmsnorm(y, fg_ref[...])
        return y

    i = pl.program_id(0)

    @pl.when(i < n_prompt_tiles)
    def _():
        for rows in _row_groups(xp_ref.shape[0], FFN_ROW_GROUPS):
            op_ref[rows, :] = ffn(xp_ref[rows, :])

    @pl.when(i == n_prompt_tiles)
    def _():
        _store_time_major(os_ref, ffn(_time_major(xs_ref)))


def _ffn(xp, xs, ng, wg, wu, wd, fg, cast_items, *, tm, final_norm):
    n, d = xp.shape
    dff = wg.shape[1]
    n_tiles = n // tm
    cast_in, cast_out, cast_shapes = _cast_specs(cast_items)
    prompt_spec = pl.BlockSpec((tm, d), lambda i: (jnp.minimum(i, n_tiles - 1), 0))
    outs = pl.pallas_call(
        functools.partial(_ffn_kernel, final_norm=final_norm, n_prompt_tiles=n_tiles,
                          n_cast=len(cast_items)),
        grid=(n_tiles + 1,),
        in_specs=[
            prompt_spec,
            _const_spec(xs.shape),
            _const_spec((1, d)),
            _const_spec((d, dff)),
            _const_spec((d, dff)),
            _const_spec((dff, d)),
            _const_spec((1, d)),
        ] + cast_in,
        out_specs=[prompt_spec, pl.BlockSpec(xs.shape, lambda i: (0, 0, 0))] + cast_out,
        out_shape=[jax.ShapeDtypeStruct((n, d), F32), jax.ShapeDtypeStruct(xs.shape, F32)] + cast_shapes,
        compiler_params=_params("arbitrary"),
        name="swiglu_ffn_final" if final_norm else "swiglu_ffn",
    )(xp, xs, ng, wg, wu, wd, fg, *[item[0] for item in cast_items])
    return outs[0], outs[1], outs[2:]


def _sgu_prompt_kernel(x_ref, ng_ref, win_ref, bin_ref, lg_ref, lb_ref, ws_ref, bs_ref,
                       wout_ref, bout_ref, o_ref, *, tl):
    c = SGU_CHUNK
    row = lax.broadcasted_iota(jnp.int32, (c, c), 0)
    col = lax.broadcasted_iota(jnp.int32, (c, c), 1)
    causal = (col <= row).astype(F32)
    w_heads = [(ws_ref[hd] * causal).astype(BF16) for hd in range(SGU_HEADS)]

    for rows in _row_groups(tl):
        n_c = rows.size // c
        x = x_ref[rows, :]
        h = _rmsnorm(x, ng_ref[...]).astype(BF16)
        z = _gelu_exact(_dot(h, win_ref[...]) + bin_ref[...])
        u = z[:, :D_MODEL]
        v = _layernorm(z[:, D_MODEL:], lg_ref[...], lb_ref[...]).astype(BF16)
        mixed_cols = []
        for hd in range(SGU_HEADS):
            lanes = slice(hd * SGU_HEAD_DIM, (hd + 1) * SGU_HEAD_DIM)
            v_h = jnp.concatenate([v[i * c:(i + 1) * c, lanes] for i in range(n_c)], axis=1)
            m_h = _dot(w_heads[hd], v_h)
            mixed_cols.append(
                jnp.concatenate([m_h[:, i * SGU_HEAD_DIM:(i + 1) * SGU_HEAD_DIM] for i in range(n_c)], axis=0))
        mixed = jnp.concatenate(mixed_cols, axis=1)
        bias = jnp.concatenate([bs_ref[...]] * n_c, axis=0)
        gated = (u * (mixed + bias)).astype(BF16)
        o_ref[rows, :] = x + _dot(gated, wout_ref[...]) + bout_ref[...]


def _sgu_prompt(x, ng, win, b_in, lg, lb, ws, bs_full, wout, bout, *, tl):
    n, d = x.shape
    return pl.pallas_call(
        functools.partial(_sgu_prompt_kernel, tl=tl),
        grid=(n // tl,),
        in_specs=[
            pl.BlockSpec((tl, d), lambda i: (i, 0)),
            _const_spec((1, d)),
            _const_spec((d, 2 * d)),
            _const_spec((1, 2 * d)),
            _const_spec((1, d)),
            _const_spec((1, d)),
            _const_spec(ws.shape),
            _const_spec(bs_full.shape),
            _const_spec((d, d)),
            _const_spec((1, d)),
        ],
        out_specs=pl.BlockSpec((tl, d), lambda i: (i, 0)),
        out_shape=jax.ShapeDtypeStruct((n, d), F32),
        compiler_params=_params("arbitrary"),
        name="sgu_mixer_prompt",
    )(x, ng, win, b_in, lg, lb, ws, bs_full, wout, bout)


def _sgu_sample_kernel(x_ref, ng_ref, win_ref, bin_ref, lg_ref, lb_ref, w4_ref, b4_ref,
                       wout_ref, bout_ref, o_ref, v_ref):
    nb, n_t, _ = x_ref.shape
    x = _time_major(x_ref)
    h = _rmsnorm(x, ng_ref[...]).astype(BF16)
    z = _gelu_exact(_dot(h, win_ref[...]) + bin_ref[...])
    u = z[:, :D_MODEL]
    v = _layernorm(z[:, D_MODEL:], lg_ref[...], lb_ref[...])
    _store_time_major(v_ref, v)
    mixed = []
    for t in range(n_t):
        m_t = jnp.broadcast_to(b4_ref[pl.ds(t, 1), :], (nb, D_MODEL))
        for s in range(t + 1):
            m_t = m_t + w4_ref[t, pl.ds(s, 1), :] * v[s * nb:(s + 1) * nb, :]
        mixed.append(m_t)
    gated = (u * jnp.concatenate(mixed, axis=0)).astype(BF16)
    _store_time_major(o_ref, x + _dot(gated, wout_ref[...]) + bout_ref[...])


def _sgu_sample(x, ng, win, b_in, lg, lb, w4, b4, wout, bout):
    whole = lambda shape: pl.BlockSpec(shape, lambda i: (0,) * len(shape))
    return pl.pallas_call(
        _sgu_sample_kernel,
        grid=(1,),
        in_specs=[whole(a.shape) for a in (x, ng, win, b_in, lg, lb, w4, b4, wout, bout)],
        out_specs=[whole(x.shape), whole(x.shape)],
        out_shape=[jax.ShapeDtypeStruct(x.shape, F32), jax.ShapeDtypeStruct(x.shape, F32)],
        compiler_params=_params("arbitrary"),
        name="sgu_mixer_sample",
    )(x, ng, win, b_in, lg, lb, w4, b4, wout, bout)


def kernel(x_prompt, x_sample, state_conv, conv_norm_g, conv_w_pw1, conv_b_pw1, conv_w_dw, conv_b_dw, conv_ln_g, conv_ln_b, conv_w_pw2, conv_b_pw2, sgu_norm_g, sgu_w_in, sgu_b_in, sgu_ln_g, sgu_ln_b, sgu_w_s, sgu_b_s, sgu_w_out, sgu_b_out, ffn_norm_g, ffn_w_gate, ffn_w_up, ffn_w_down, final_norm_g):
    batch, seq, d = x_prompt.shape
    n_t = x_sample.shape[1]
    row = lambda a: a.reshape(1, -1)
    ffn_items = lambda i: [(w, i) for w in (ffn_w_gate, ffn_w_up, ffn_w_down)]

    xp = x_prompt.reshape(batch * seq, d)
    st_t = jnp.transpose(state_conv[0], (1, 0, 2))

    w1 = conv_w_pw1[0].astype(BF16)
    w2 = conv_w_pw2[0].astype(BF16)
    wdw = conv_w_dw[0]
    wdw_c = jnp.pad(wdw, ((0, CTX_PAD - CONV_WIDTH), (0, 0))).reshape(CTX_PAD, N_LANE_CHUNKS, LANES)
    wdw_c = jnp.transpose(wdw_c, (1, 0, 2))
    bdw_c = conv_b_dw[0].reshape(N_LANE_CHUNKS, 1, LANES)
    conv_args = (row(conv_norm_g[0]), w1, row(conv_b_pw1[0]))
    conv_tail = (row(conv_ln_g[0]), row(conv_ln_b[0]), w2, row(conv_b_pw2[0]))

    xp, conv_p, ffn0_bf16 = _conv_prompt(xp, *conv_args, wdw_c, bdw_c, *conv_tail, ffn_items(0),
                                         batch=batch, seq=seq, tl=TOKEN_TILE)
    xs, conv_s_t = _conv_sample(x_sample, st_t, *conv_args, wdw, row(conv_b_dw[0]), *conv_tail, bc=32)

    fg = row(final_norm_g)
    later_items = [(sgu_w_in, 0), (sgu_w_out, 0)] + ffn_items(1)
    xp, xs, later_bf16 = _ffn(xp, xs, row(ffn_norm_g[0]), *ffn0_bf16, fg, later_items,
                              tm=TOKEN_TILE, final_norm=False)

    win, wout = later_bf16[:2]
    sgu_args = (row(sgu_norm_g[0]), win, row(sgu_b_in[0]), row(sgu_ln_g[0]), row(sgu_ln_b[0]))
    bs_full = jnp.repeat(jnp.transpose(sgu_b_s[0]), SGU_HEAD_DIM, axis=1)
    w4 = jnp.repeat(jnp.transpose(sgu_w_s[0][:, :n_t, :n_t], (1, 2, 0)), SGU_HEAD_DIM, axis=2)
    b4 = bs_full[:n_t]

    xp = _sgu_prompt(xp, *sgu_args, sgu_w_s[0], bs_full, wout, row(sgu_b_out[0]), tl=TOKEN_TILE)
    xs, v_s = _sgu_sample(xs, *sgu_args, w4, b4, wout, row(sgu_b_out[0]))

    yp, ys, _ = _ffn(xp, xs, row(ffn_norm_g[1]), *later_bf16[2:], fg, [], tm=TOKEN_TILE, final_norm=True)

    new_conv_sample = jnp.transpose(conv_s_t, (1, 0, 2))[None]
    return (yp.reshape(batch, seq, d), ys, conv_p[None], new_conv_sample, v_s[None])
```

```python
import functools

import jax
import jax.numpy as jnp
from jax import lax
from jax.experimental import pallas as pl
from jax.experimental.pallas import tpu as pltpu

D_MODEL = 1024
CONV_WIDTH = 31
CTX = CONV_WIDTH - 1
SGU_HEADS = 8
SGU_HEAD_DIM = D_MODEL // SGU_HEADS
SGU_CHUNK = 128
RMS_EPS = 1e-6
LN_EPS = 1e-5

LANES = 128
MXU_COLS = 256
BF16_SUBLANES = 16
N_LANE_CHUNKS = D_MODEL // LANES
CTX_PAD = 32
VMEM_LIMIT = 56 * 1024 * 1024
TOKEN_TILE = 512
FFN_ROW_GROUPS = 2
CAST_SLABS = 8
CAST_EVERY = 4

F32 = jnp.float32
BF16 = jnp.bfloat16


def _rmsnorm(x, g):
    return x * lax.rsqrt(jnp.mean(x * x, axis=-1, keepdims=True) + RMS_EPS) * g


def _layernorm(x, g, b):
    mu = jnp.mean(x, axis=-1, keepdims=True)
    xc = x - mu
    var = jnp.mean(xc * xc, axis=-1, keepdims=True)
    return xc * lax.rsqrt(var + LN_EPS) * g + b


def _sigmoid(x):
    return 1.0 / (1.0 + jnp.exp(-x))


def _silu(x):
    return x * _sigmoid(x)


def _gelu_exact(x):
    return 0.5 * x * (1.0 + lax.erf(x * (2.0 ** -0.5)))


def _dot(a, b):
    return jnp.dot(a, b, preferred_element_type=F32)


def _const_spec(shape):
    nd = len(shape)
    return pl.BlockSpec(shape, lambda *_: (0,) * nd, pipeline_mode=pl.Buffered(1))


def _params(*sem):
    return pltpu.CompilerParams(dimension_semantics=sem, vmem_limit_bytes=VMEM_LIMIT)


def _row_groups(n_rows, n_groups=1):
    size = n_rows // n_groups
    return [pl.ds(g * size, size) for g in range(n_groups)]


def _time_major(x_ref):
    return jnp.concatenate([x_ref[:, t, :] for t in range(x_ref.shape[1])], axis=0)


def _store_time_major(o_ref, y):
    nb, n_t, _ = o_ref.shape
    for t in range(n_t):
        o_ref[:, t, :] = y[t * nb:(t + 1) * nb, :]


def _cast_specs(items):
    slab_of = lambda s: jnp.minimum(s // CAST_EVERY, CAST_SLABS - 1)
    in_specs, out_specs, shapes = [], [], []
    for stacked, layer in items:
        _, rows, cols = stacked.shape
        assert rows % (CAST_SLABS * BF16_SUBLANES) == 0
        slab = rows // CAST_SLABS
        in_specs.append(pl.BlockSpec((None, slab, cols), lambda s, layer=layer: (layer, slab_of(s), 0)))
        out_specs.append(pl.BlockSpec((slab, cols), lambda s: (slab_of(s), 0)))
        shapes.append(jax.ShapeDtypeStruct((rows, cols), BF16))
    return in_specs, out_specs, shapes


def _cast_slabs(src_refs, dst_refs):
    if not src_refs:
        return

    @pl.when(pl.program_id(0) % CAST_EVERY == 0)
    def _():
        for s, d in zip(src_refs, dst_refs):
            d[...] = s[...].astype(BF16)


def _conv_prompt_kernel(x_ref, ng_ref, w1_ref, b1_ref, wdw_ref, bdw_ref, lg_ref, lb_ref,
                        w2_ref, b2_ref, *rest, tl, tiles_per_seq, n_cast):
    cast_src = rest[:n_cast]
    o_ref, ctx_ref = rest[n_cast:n_cast + 2]
    cast_dst = rest[n_cast + 2:2 * n_cast + 2]
    fbuf, cbuf = rest[2 * n_cast + 2:]
    _cast_slabs(cast_src, cast_dst)
    l = pl.program_id(0) % tiles_per_seq

    @pl.when(l == 0)
    def _():
        fbuf[:, pl.ds(0, CTX_PAD), :] = jnp.zeros((N_LANE_CHUNKS, CTX_PAD, LANES), F32)

    @pl.when(l > 0)
    def _():
        fbuf[:, pl.ds(0, CTX_PAD), :] = fbuf[:, pl.ds(tl, CTX_PAD), :]

    h = _rmsnorm(x_ref[...], ng_ref[...]).astype(BF16)
    for j in range(N_LANE_CHUNKS):
        a = _dot(h, w1_ref[j]) + b1_ref[j]
        fbuf[j, pl.ds(CTX_PAD, tl), :] = a[:, :LANES] * _sigmoid(a[:, LANES:])

    shift0 = CTX_PAD - CTX
    row_chunk = 64

    def lane_chunk(j, carry):
        bias = bdw_ref[j]
        for r0 in range(0, tl, row_chunk):
            acc = jnp.broadcast_to(bias, (row_chunk, LANES))
            for k in range(CONV_WIDTH):
                w_k = wdw_ref[j, pl.ds(k, 1), :]
                acc = acc + w_k * fbuf[j, pl.ds(r0 + k + shift0, row_chunk), :]
            cbuf[j, pl.ds(r0, row_chunk), :] = acc
        return carry

    lax.fori_loop(0, N_LANE_CHUNKS, lane_chunk, 0)

    conv = jnp.concatenate([cbuf[j] for j in range(N_LANE_CHUNKS)], axis=1)
    y = _silu(_layernorm(conv, lg_ref[...], lb_ref[...])).astype(BF16)
    for k in range(D_MODEL // MXU_COLS):
        cols = slice(k * MXU_COLS, (k + 1) * MXU_COLS)
        o_ref[:, cols] = x_ref[:, cols] + _dot(y, w2_ref[:, cols]) + b2_ref[:, cols]

    @pl.when(l == tiles_per_seq - 1)
    def _():
        for j in range(N_LANE_CHUNKS):
            ctx_ref[0, :, j * LANES:(j + 1) * LANES] = fbuf[j, pl.ds(tl + CTX_PAD - CTX, CTX), :]


def _conv_prompt(x, ng, w1, b1, wdw_c, bdw_c, lg, lb, w2, b2, cast_items, *, batch, seq, tl):
    tiles_per_seq = seq // tl
    d = D_MODEL
    cast_in, cast_out, cast_shapes = _cast_specs(cast_items)
    outs = pl.pallas_call(
        functools.partial(_conv_prompt_kernel, tl=tl, tiles_per_seq=tiles_per_seq, n_cast=len(cast_items)),
        grid=(batch * tiles_per_seq,),
        in_specs=[
            pl.BlockSpec((tl, d), lambda s: (s, 0)),
            _const_spec((1, d)),
            _const_spec(w1.shape),
            _const_spec(b1.shape),
            _const_spec(wdw_c.shape),
            _const_spec(bdw_c.shape),
            _const_spec((1, d)),
            _const_spec((1, d)),
            _const_spec((d, d)),
            _const_spec((1, d)),
        ] + cast_in,
        out_specs=[
            pl.BlockSpec((tl, d), lambda s: (s, 0)),
            pl.BlockSpec((1, CTX, d), lambda s: (s // tiles_per_seq, 0, 0)),
        ] + cast_out,
        out_shape=[
            jax.ShapeDtypeStruct((batch * seq, d), F32),
            jax.ShapeDtypeStruct((batch, CTX, d), F32),
        ] + cast_shapes,
        scratch_shapes=[
            pltpu.VMEM((N_LANE_CHUNKS, tl + CTX_PAD, LANES), F32),
            pltpu.VMEM((N_LANE_CHUNKS, tl, LANES), F32),
        ],
        compiler_params=_params("arbitrary"),
        name="conv_mixer_prompt",
    )(x, ng, w1, b1, wdw_c, bdw_c, lg, lb, w2, b2, *[item[0] for item in cast_items])
    return outs[0], outs[1], outs[2:]


def _conv_sample_kernel(x_ref, st_ref, ng_ref, w1_ref, b1_ref, wdw_ref, bdw_ref, lg_ref, lb_ref,
                        w2_ref, b2_ref, o_ref, nst_ref, gbuf, cbuf, *, n_t, bc):
    x = _time_major(x_ref)
    h = _rmsnorm(x, ng_ref[...]).astype(BF16)
    a = _dot(h, w1_ref[...]) + b1_ref[...]
    glu = a[:, :D_MODEL] * _sigmoid(a[:, D_MODEL:])
    for t in range(n_t):
        gbuf[t] = glu[t * bc:(t + 1) * bc, :]

    nst_ref[pl.ds(0, CTX - n_t)] = st_ref[pl.ds(n_t, CTX - n_t)]
    nst_ref[pl.ds(CTX - n_t, n_t)] = gbuf[...]

    def lane_chunk(j, carry):
        lanes = pl.ds(pl.multiple_of(j * LANES, LANES), LANES)
        bias = bdw_ref[:, lanes]
        accs = [jnp.broadcast_to(bias, (bc, LANES)) for _ in range(n_t)]
        for m in range(CTX + n_t):
            f = st_ref[m, :, lanes] if m < CTX else gbuf[m - CTX, :, lanes]
            for t in range(n_t):
                k = m - t
                if 0 <= k < CONV_WIDTH:
                    accs[t] = accs[t] + wdw_ref[pl.ds(k, 1), lanes] * f
        for t in range(n_t):
            cbuf[t, :, lanes] = accs[t]
        return carry

    lax.fori_loop(0, N_LANE_CHUNKS, lane_chunk, 0)

    conv = jnp.concatenate([cbuf[t] for t in range(n_t)], axis=0)
    y = _silu(_layernorm(conv, lg_ref[...], lb_ref[...])).astype(BF16)
    _store_time_major(o_ref, x + _dot(y, w2_ref[...]) + b2_ref[...])


def _conv_sample(x, st_t, ng, w1, b1, wdw, bdw, lg, lb, w2, b2, *, bc):
    nb, n_t, d = x.shape
    return pl.pallas_call(
        functools.partial(_conv_sample_kernel, n_t=n_t, bc=bc),
        grid=(nb // bc,),
        in_specs=[
            pl.BlockSpec((bc, n_t, d), lambda i: (i, 0, 0)),
            pl.BlockSpec((CTX, bc, d), lambda i: (0, i, 0)),
            _const_spec((1, d)),
            _const_spec((d, 2 * d)),
            _const_spec((1, 2 * d)),
            _const_spec(wdw.shape),
            _const_spec((1, d)),
            _const_spec((1, d)),
            _const_spec((1, d)),
            _const_spec((d, d)),
            _const_spec((1, d)),
        ],
        out_specs=[
            pl.BlockSpec((bc, n_t, d), lambda i: (i, 0, 0)),
            pl.BlockSpec((CTX, bc, d), lambda i: (0, i, 0)),
        ],
        out_shape=[
            jax.ShapeDtypeStruct((nb, n_t, d), F32),
            jax.ShapeDtypeStruct((CTX, nb, d), F32),
        ],
        scratch_shapes=[
            pltpu.VMEM((n_t, bc, d), F32),
            pltpu.VMEM((n_t, bc, d), F32),
        ],
        compiler_params=_params("arbitrary"),
        name="conv_mixer_sample",
    )(x, st_t, ng, w1, b1, wdw, bdw, lg, lb, w2, b2)


def _ffn_kernel(xp_ref, xs_ref, ng_ref, wg_ref, wu_ref, wd_ref, fg_ref, *rest,
                final_norm, n_prompt_tiles, n_cast):
    cast_src = rest[:n_cast]
    op_ref, os_ref = rest[n_cast:n_cast + 2]
    cast_dst = rest[n_cast + 2:]
    _cast_slabs(cast_src, cast_dst)

    def ffn(x):
        h = _rmsnorm(x, ng_ref[...]).astype(BF16)
        acts = []
        for k in range(wg_ref.shape[1] // MXU_COLS):
            cols = slice(k * MXU_COLS, (k + 1) * MXU_COLS)
            acts.append((_silu(_dot(h, wg_ref[:, cols])) * _dot(h, wu_ref[:, cols])).astype(BF16))
        y = x + _dot(jnp.concatenate(acts, axis=1), wd_ref[...])
        if final_norm:
            y = _rmsnorm(y, fg_ref[...])
        return y

    i = pl.program_id(0)

    @pl.when(i < n_prompt_tiles)
    def _():
        for rows in _row_groups(xp_ref.shape[0], FFN_ROW_GROUPS):
            op_ref[rows, :] = ffn(xp_ref[rows, :])

    @pl.when(i == n_prompt_tiles)
    def _():
        _store_time_major(os_ref, ffn(_time_major(xs_ref)))


def _ffn(xp, xs, ng, wg, wu, wd, fg, cast_items, *, tm, final_norm):
    n, d = xp.shape
    dff = wg.shape[1]
    n_tiles = n // tm
    cast_in, cast_out, cast_shapes = _cast_specs(cast_items)
    prompt_spec = pl.BlockSpec((tm, d), lambda i: (jnp.minimum(i, n_tiles - 1), 0))
    outs = pl.pallas_call(
        functools.partial(_ffn_kernel, final_norm=final_norm, n_prompt_tiles=n_tiles,
                          n_cast=len(cast_items)),
        grid=(n_tiles + 1,),
        in_specs=[
            prompt_spec,
            _const_spec(xs.shape),
            _const_spec((1, d)),
            _const_spec((d, dff)),
            _const_spec((d, dff)),
            _const_spec((dff, d)),
            _const_spec((1, d)),
        ] + cast_in,
        out_specs=[prompt_spec, pl.BlockSpec(xs.shape, lambda i: (0, 0, 0))] + cast_out,
        out_shape=[jax.ShapeDtypeStruct((n, d), F32), jax.ShapeDtypeStruct(xs.shape, F32)] + cast_shapes,
        compiler_params=_params("arbitrary"),
        name="swiglu_ffn_final" if final_norm else "swiglu_ffn",
    )(xp, xs, ng, wg, wu, wd, fg, *[item[0] for item in cast_items])
    return outs[0], outs[1], outs[2:]


def _sgu_prompt_kernel(x_ref, ng_ref, win_ref, bin_ref, lg_ref, lb_ref, ws_ref, bs_ref,
                       wout_ref, bout_ref, o_ref, *, tl):
    c = SGU_CHUNK
    row = lax.broadcasted_iota(jnp.int32, (c, c), 0)
    col = lax.broadcasted_iota(jnp.int32, (c, c), 1)
    causal = (col <= row).astype(F32)
    w_heads = [(ws_ref[hd] * causal).astype(BF16) for hd in range(SGU_HEADS)]

    n_c = tl // c
    x = x_ref[...]
    h = _rmsnorm(x, ng_ref[...]).astype(BF16)

    def in_proj(cols):
        return _gelu_exact(_dot(h, win_ref[:, cols]) + bin_ref[:, cols])

    col_blocks = [slice(k * MXU_COLS, (k + 1) * MXU_COLS) for k in range(2 * D_MODEL // MXU_COLS)]
    half = len(col_blocks) // 2
    v_raw = jnp.concatenate([in_proj(cols) for cols in col_blocks[half:]], axis=1)
    v = _layernorm(v_raw, lg_ref[...], lb_ref[...]).astype(BF16)
    u = jnp.concatenate([in_proj(cols) for cols in col_blocks[:half]], axis=1)

    gated_cols = []
    for hd in range(SGU_HEADS):
        lanes = slice(hd * SGU_HEAD_DIM, (hd + 1) * SGU_HEAD_DIM)
        v_h = jnp.concatenate([v[i * c:(i + 1) * c, lanes] for i in range(n_c)], axis=1)
        m_h = _dot(w_heads[hd], v_h)
        bias_h = bs_ref[:, lanes]
        gated_cols.append(jnp.concatenate(
            [u[i * c:(i + 1) * c, lanes] * (m_h[:, i * SGU_HEAD_DIM:(i + 1) * SGU_HEAD_DIM] + bias_h)
             for i in range(n_c)], axis=0))
    gated = jnp.concatenate(gated_cols, axis=1).astype(BF16)
    for cols in col_blocks[:half]:
        o_ref[:, cols] = x[:, cols] + _dot(gated, wout_ref[:, cols]) + bout_ref[:, cols]


def _sgu_prompt(x, ng, win, b_in, lg, lb, ws, bs_full, wout, bout, *, tl):
    n, d = x.shape
    return pl.pallas_call(
        functools.partial(_sgu_prompt_kernel, tl=tl),
        grid=(n // tl,),
        in_specs=[
            pl.BlockSpec((tl, d), lambda i: (i, 0)),
            _const_spec((1, d)),
            _const_spec((d, 2 * d)),
            _const_spec((1, 2 * d)),
            _const_spec((1, d)),
            _const_spec((1, d)),
            _const_spec(ws.shape),
            _const_spec(bs_full.shape),
            _const_spec((d, d)),
            _const_spec((1, d)),
        ],
        out_specs=pl.BlockSpec((tl, d), lambda i: (i, 0)),
        out_shape=jax.ShapeDtypeStruct((n, d), F32),
        compiler_params=_params("arbitrary"),
        name="sgu_mixer_prompt",
    )(x, ng, win, b_in, lg, lb, ws, bs_full, wout, bout)


def _sgu_sample_kernel(x_ref, ng_ref, win_ref, bin_ref, lg_ref, lb_ref, w4_ref, b4_ref,
                       wout_ref, bout_ref, o_ref, v_ref):
    nb, n_t, _ = x_ref.shape
    x = _time_major(x_ref)
    h = _rmsnorm(x, ng_ref[...]).astype(BF16)
    z = _gelu_exact(_dot(h, win_ref[...]) + bin_ref[...])
    u = z[:, :D_MODEL]
    v = _layernorm(z[:, D_MODEL:], lg_ref[...], lb_ref[...])
    _store_time_major(v_ref, v)
    mixed = []
    for t in range(n_t):
        m_t = jnp.broadcast_to(b4_ref[pl.ds(t, 1), :], (nb, D_MODEL))
        for s in range(t + 1):
            m_t = m_t + w4_ref[t, pl.ds(s, 1), :] * v[s * nb:(s + 1) * nb, :]
        mixed.append(m_t)
    gated = (u * jnp.concatenate(mixed, axis=0)).astype(BF16)
    _store_time_major(o_ref, x + _dot(gated, wout_ref[...]) + bout_ref[...])


def _sgu_sample(x, ng, win, b_in, lg, lb, w4, b4, wout, bout):
    whole = lambda shape: pl.BlockSpec(shape, lambda i: (0,) * len(shape))
    return pl.pallas_call(
        _sgu_sample_kernel,
        grid=(1,),
        in_specs=[whole(a.shape) for a in (x, ng, win, b_in, lg, lb, w4, b4, wout, bout)],
        out_specs=[whole(x.shape), whole(x.shape)],
        out_shape=[jax.ShapeDtypeStruct(x.shape, F32), jax.ShapeDtypeStruct(x.shape, F32)],
        compiler_params=_params("arbitrary"),
        name="sgu_mixer_sample",
    )(x, ng, win, b_in, lg, lb, w4, b4, wout, bout)


def kernel(x_prompt, x_sample, state_conv, conv_norm_g, conv_w_pw1, conv_b_pw1, conv_w_dw, conv_b_dw, conv_ln_g, conv_ln_b, conv_w_pw2, conv_b_pw2, sgu_norm_g, sgu_w_in, sgu_b_in, sgu_ln_g, sgu_ln_b, sgu_w_s, sgu_b_s, sgu_w_out, sgu_b_out, ffn_norm_g, ffn_w_gate, ffn_w_up, ffn_w_down, final_norm_g):
    batch, seq, d = x_prompt.shape
    n_t = x_sample.shape[1]
    row = lambda a: a.reshape(1, -1)
    ffn_items = lambda i: [(w, i) for w in (ffn_w_gate, ffn_w_up, ffn_w_down)]

    xp = x_prompt.reshape(batch * seq, d)
    st_t = jnp.transpose(state_conv[0], (1, 0, 2))

    w1 = conv_w_pw1[0].astype(BF16)
    w2 = conv_w_pw2[0].astype(BF16)
    wdw = conv_w_dw[0]
    wdw_c = jnp.pad(wdw, ((0, CTX_PAD - CONV_WIDTH), (0, 0))).reshape(CTX_PAD, N_LANE_CHUNKS, LANES)
    wdw_c = jnp.transpose(wdw_c, (1, 0, 2))
    bdw_c = conv_b_dw[0].reshape(N_LANE_CHUNKS, 1, LANES)
    conv_args = (row(conv_norm_g[0]), w1, row(conv_b_pw1[0]))
    conv_tail = (row(conv_ln_g[0]), row(conv_ln_b[0]), w2, row(conv_b_pw2[0]))

    w1r = jnp.transpose(w1.reshape(d, 2, N_LANE_CHUNKS, LANES), (2, 0, 1, 3)).reshape(N_LANE_CHUNKS, d, 2 * LANES)
    b1r = jnp.transpose(conv_b_pw1[0].reshape(2, N_LANE_CHUNKS, LANES), (1, 0, 2)).reshape(N_LANE_CHUNKS, 1, 2 * LANES)
    xp, conv_p, ffn0_bf16 = _conv_prompt(xp, conv_args[0], w1r, b1r, wdw_c, bdw_c, *conv_tail, ffn_items(0),
                                         batch=batch, seq=seq, tl=TOKEN_TILE)
    xs, conv_s_t = _conv_sample(x_sample, st_t, *conv_args, wdw, row(conv_b_dw[0]), *conv_tail, bc=32)

    fg = row(final_norm_g)
    later_items = [(sgu_w_in, 0), (sgu_w_out, 0)] + ffn_items(1)
    xp, xs, later_bf16 = _ffn(xp, xs, row(ffn_norm_g[0]), *ffn0_bf16, fg, later_items,
                              tm=TOKEN_TILE, final_norm=False)

    win, wout = later_bf16[:2]
    sgu_args = (row(sgu_norm_g[0]), win, row(sgu_b_in[0]), row(sgu_ln_g[0]), row(sgu_ln_b[0]))
    bs_full = jnp.repeat(jnp.transpose(sgu_b_s[0]), SGU_HEAD_DIM, axis=1)
    w4 = jnp.repeat(jnp.transpose(sgu_w_s[0][:, :n_t, :n_t], (1, 2, 0)), SGU_HEAD_DIM, axis=2)
    b4 = bs_full[:n_t]

    xp = _sgu_prompt(xp, *sgu_args, sgu_w_s[0], bs_full, wout, row(sgu_b_out[0]), tl=TOKEN_TILE)
    xs, v_s = _sgu_sample(xs, *sgu_args, w4, b4, wout, row(sgu_b_out[0]))

    yp, ys, _ = _ffn(xp, xs, row(ffn_norm_g[1]), *later_bf16[2:], fg, [], tm=TOKEN_TILE, final_norm=True)

    new_conv_sample = jnp.transpose(conv_s_t, (1, 0, 2))[None]
    return (yp.reshape(batch, seq, d), ys, conv_p[None], new_conv_sample, v_s[None])
```

```python
import functools

import jax
import jax.numpy as jnp
from jax import lax
from jax.experimental import pallas as pl
from jax.experimental.pallas import tpu as pltpu

D_MODEL = 1024
CONV_WIDTH = 31
CTX = CONV_WIDTH - 1
SGU_HEADS = 8
SGU_HEAD_DIM = D_MODEL // SGU_HEADS
SGU_CHUNK = 128
RMS_EPS = 1e-6
LN_EPS = 1e-5

LANES = 128
MXU_COLS = 256
BF16_SUBLANES = 16
N_LANE_CHUNKS = D_MODEL // LANES
CTX_PAD = 32
VMEM_LIMIT = 56 * 1024 * 1024
TOKEN_TILE = 512
FFN_ROW_GROUPS = 2
FFN_TOKEN_TILE = 1024
CAST_SLABS = 8

F32 = jnp.float32
BF16 = jnp.bfloat16


def _rmsnorm(x, g):
    return x * lax.rsqrt(jnp.mean(x * x, axis=-1, keepdims=True) + RMS_EPS) * g


def _layernorm(x, g, b):
    mu = jnp.mean(x, axis=-1, keepdims=True)
    xc = x - mu
    var = jnp.mean(xc * xc, axis=-1, keepdims=True)
    return xc * lax.rsqrt(var + LN_EPS) * g + b


def _sigmoid(x):
    return 1.0 / (1.0 + jnp.exp(-x))


def _silu(x):
    return x * _sigmoid(x)


def _gelu_exact(x):
    return 0.5 * x * (1.0 + lax.erf(x * (2.0 ** -0.5)))


def _dot(a, b):
    return jnp.dot(a, b, preferred_element_type=F32)


def _const_spec(shape):
    nd = len(shape)
    return pl.BlockSpec(shape, lambda *_: (0,) * nd, pipeline_mode=pl.Buffered(1))


def _params(*sem):
    return pltpu.CompilerParams(dimension_semantics=sem, vmem_limit_bytes=VMEM_LIMIT)


def _row_groups(n_rows, n_groups=1):
    size = n_rows // n_groups
    return [pl.ds(g * size, size) for g in range(n_groups)]


def _time_major(x_ref):
    return jnp.concatenate([x_ref[:, t, :] for t in range(x_ref.shape[1])], axis=0)


def _store_time_major(o_ref, y):
    nb, n_t, _ = o_ref.shape
    for t in range(n_t):
        o_ref[:, t, :] = y[t * nb:(t + 1) * nb, :]


def _cast_specs(items, every):
    slab_of = lambda s: jnp.minimum(s // every, CAST_SLABS - 1)
    in_specs, out_specs, shapes = [], [], []
    for stacked, layer in items:
        _, rows, cols = stacked.shape
        assert rows % (CAST_SLABS * BF16_SUBLANES) == 0
        slab = rows // CAST_SLABS
        in_specs.append(pl.BlockSpec((None, slab, cols), lambda s, layer=layer: (layer, slab_of(s), 0)))
        out_specs.append(pl.BlockSpec((slab, cols), lambda s: (slab_of(s), 0)))
        shapes.append(jax.ShapeDtypeStruct((rows, cols), BF16))
    return in_specs, out_specs, shapes


def _cast_slabs(src_refs, dst_refs, every):
    if not src_refs:
        return

    @pl.when(pl.program_id(0) % every == 0)
    def _():
        for s, d in zip(src_refs, dst_refs):
            d[...] = s[...].astype(BF16)


def _conv_prompt_kernel(x_ref, ng_ref, w1_ref, b1_ref, wdw_ref, bdw_ref, lg_ref, lb_ref,
                        w2_ref, b2_ref, *rest, tl, tiles_per_seq, n_cast, cast_every):
    cast_src = rest[:n_cast]
    o_ref, ctx_ref = rest[n_cast:n_cast + 2]
    cast_dst = rest[n_cast + 2:2 * n_cast + 2]
    fbuf, cbuf = rest[2 * n_cast + 2:]
    _cast_slabs(cast_src, cast_dst, cast_every)
    l = pl.program_id(0) % tiles_per_seq

    @pl.when(l == 0)
    def _():
        fbuf[:, pl.ds(0, CTX_PAD), :] = jnp.zeros((N_LANE_CHUNKS, CTX_PAD, LANES), F32)

    @pl.when(l > 0)
    def _():
        fbuf[:, pl.ds(0, CTX_PAD), :] = fbuf[:, pl.ds(tl, CTX_PAD), :]

    h = _rmsnorm(x_ref[...], ng_ref[...]).astype(BF16)
    for j in range(N_LANE_CHUNKS):
        value_cols = slice(j * LANES, (j + 1) * LANES)
        gate_cols = slice(D_MODEL + j * LANES, D_MODEL + (j + 1) * LANES)
        w_pair = jnp.concatenate([w1_ref[:, value_cols], w1_ref[:, gate_cols]], axis=1)
        b_pair = jnp.concatenate([b1_ref[:, value_cols], b1_ref[:, gate_cols]], axis=1)
        a = _dot(h, w_pair) + b_pair
        fbuf[j, pl.ds(CTX_PAD, tl), :] = a[:, :LANES] * _sigmoid(a[:, LANES:])

    shift0 = CTX_PAD - CTX
    row_chunk = 64

    def lane_chunk(j, carry):
        bias = bdw_ref[j]
        for r0 in range(0, tl, row_chunk):
            acc = jnp.broadcast_to(bias, (row_chunk, LANES))
            for k in range(CONV_WIDTH):
                w_k = wdw_ref[j, pl.ds(k, 1), :]
                acc = acc + w_k * fbuf[j, pl.ds(r0 + k + shift0, row_chunk), :]
            cbuf[j, pl.ds(r0, row_chunk), :] = acc
        return carry

    lax.fori_loop(0, N_LANE_CHUNKS, lane_chunk, 0)

    conv = jnp.concatenate([cbuf[j] for j in range(N_LANE_CHUNKS)], axis=1)
    y = _silu(_layernorm(conv, lg_ref[...], lb_ref[...])).astype(BF16)
    for k in range(D_MODEL // MXU_COLS):
        cols = slice(k * MXU_COLS, (k + 1) * MXU_COLS)
        o_ref[:, cols] = x_ref[:, cols] + _dot(y, w2_ref[:, cols]) + b2_ref[:, cols]

    @pl.when(l == tiles_per_seq - 1)
    def _():
        for j in range(N_LANE_CHUNKS):
            ctx_ref[0, :, j * LANES:(j + 1) * LANES] = fbuf[j, pl.ds(tl + CTX_PAD - CTX, CTX), :]


def _conv_prompt(x, ng, w1, b1, wdw_c, bdw_c, lg, lb, w2, b2, cast_items, *, batch, seq, tl):
    tiles_per_seq = seq // tl
    n_tiles = batch * tiles_per_seq
    d = D_MODEL
    cast_every = n_tiles // CAST_SLABS
    cast_in, cast_out, cast_shapes = _cast_specs(cast_items, cast_every)
    outs = pl.pallas_call(
        functools.partial(_conv_prompt_kernel, tl=tl, tiles_per_seq=tiles_per_seq, n_cast=len(cast_items),
                          cast_every=cast_every),
        grid=(n_tiles,),
        in_specs=[
            pl.BlockSpec((tl, d), lambda s: (s, 0)),
            _const_spec((1, d)),
            _const_spec(w1.shape),
            _const_spec(b1.shape),
            _const_spec(wdw_c.shape),
            _const_spec(bdw_c.shape),
            _const_spec((1, d)),
            _const_spec((1, d)),
            _const_spec((d, d)),
            _const_spec((1, d)),
        ] + cast_in,
        out_specs=[
            pl.BlockSpec((tl, d), lambda s: (s, 0)),
            pl.BlockSpec((1, CTX, d), lambda s: (s // tiles_per_seq, 0, 0)),
        ] + cast_out,
        out_shape=[
            jax.ShapeDtypeStruct((batch * seq, d), F32),
            jax.ShapeDtypeStruct((batch, CTX, d), F32),
        ] + cast_shapes,
        scratch_shapes=[
            pltpu.VMEM((N_LANE_CHUNKS, tl + CTX_PAD, LANES), F32),
            pltpu.VMEM((N_LANE_CHUNKS, tl, LANES), F32),
        ],
        compiler_params=_params("arbitrary"),
        name="conv_mixer_prompt",
    )(x, ng, w1, b1, wdw_c, bdw_c, lg, lb, w2, b2, *[item[0] for item in cast_items])
    return outs[0], outs[1], outs[2:]


def _conv_sample_kernel(x_ref, st_ref, ng_ref, w1_ref, b1_ref, wdw_ref, bdw_ref, lg_ref, lb_ref,
                        w2_ref, b2_ref, o_ref, nst_ref, gbuf, cbuf, *, n_t, bc):
    x = _time_major(x_ref)
    h = _rmsnorm(x, ng_ref[...]).astype(BF16)
    a = _dot(h, w1_ref[...]) + b1_ref[...]
    glu = a[:, :D_MODEL] * _sigmoid(a[:, D_MODEL:])
    for t in range(n_t):
        gbuf[t] = glu[t * bc:(t + 1) * bc, :]

    nst_ref[pl.ds(0, CTX - n_t)] = st_ref[pl.ds(n_t, CTX - n_t)]
    nst_ref[pl.ds(CTX - n_t, n_t)] = gbuf[...]

    def lane_chunk(j, carry):
        lanes = pl.ds(pl.multiple_of(j * LANES, LANES), LANES)
        bias = bdw_ref[:, lanes]
        accs = [jnp.broadcast_to(bias, (bc, LANES)) for _ in range(n_t)]
        for m in range(CTX + n_t):
            f = st_ref[m, :, lanes] if m < CTX else gbuf[m - CTX, :, lanes]
            for t in range(n_t):
                k = m - t
                if 0 <= k < CONV_WIDTH:
                    accs[t] = accs[t] + wdw_ref[pl.ds(k, 1), lanes] * f
        for t in range(n_t):
            cbuf[t, :, lanes] = accs[t]
        return carry

    lax.fori_loop(0, N_LANE_CHUNKS, lane_chunk, 0)

    conv = jnp.concatenate([cbuf[t] for t in range(n_t)], axis=0)
    y = _silu(_layernorm(conv, lg_ref[...], lb_ref[...])).astype(BF16)
    _store_time_major(o_ref, x + _dot(y, w2_ref[...]) + b2_ref[...])


def _conv_sample(x, st_t, ng, w1, b1, wdw, bdw, lg, lb, w2, b2, *, bc):
    nb, n_t, d = x.shape
    return pl.pallas_call(
        functools.partial(_conv_sample_kernel, n_t=n_t, bc=bc),
        grid=(nb // bc,),
        in_specs=[
            pl.BlockSpec((bc, n_t, d), lambda i: (i, 0, 0)),
            pl.BlockSpec((CTX, bc, d), lambda i: (0, i, 0)),
            _const_spec((1, d)),
            _const_spec((d, 2 * d)),
            _const_spec((1, 2 * d)),
            _const_spec(wdw.shape),
            _const_spec((1, d)),
            _const_spec((1, d)),
            _const_spec((1, d)),
            _const_spec((d, d)),
            _const_spec((1, d)),
        ],
        out_specs=[
            pl.BlockSpec((bc, n_t, d), lambda i: (i, 0, 0)),
            pl.BlockSpec((CTX, bc, d), lambda i: (0, i, 0)),
        ],
        out_shape=[
            jax.ShapeDtypeStruct((nb, n_t, d), F32),
            jax.ShapeDtypeStruct((CTX, nb, d), F32),
        ],
        scratch_shapes=[
            pltpu.VMEM((n_t, bc, d), F32),
            pltpu.VMEM((n_t, bc, d), F32),
        ],
        compiler_params=_params("arbitrary"),
        name="conv_mixer_sample",
    )(x, st_t, ng, w1, b1, wdw, bdw, lg, lb, w2, b2)


def _ffn_kernel(xp_ref, xs_ref, ng_ref, wg_ref, wu_ref, wd_ref, fg_ref, *rest,
                final_norm, n_prompt_tiles, n_cast, cast_every):
    cast_src = rest[:n_cast]
    op_ref, os_ref = rest[n_cast:n_cast + 2]
    cast_dst = rest[n_cast + 2:]
    _cast_slabs(cast_src, cast_dst, cast_every)

    def ffn(x):
        h = _rmsnorm(x, ng_ref[...]).astype(BF16)
        acts = []
        for k in range(wg_ref.shape[1] // MXU_COLS):
            cols = slice(k * MXU_COLS, (k + 1) * MXU_COLS)
            acts.append((_silu(_dot(h, wg_ref[:, cols])) * _dot(h, wu_ref[:, cols])).astype(BF16))
        y = x + _dot(jnp.concatenate(acts, axis=1), wd_ref[...])
        if final_norm:
            y = _rmsnorm(y, fg_ref[...])
        return y

    i = pl.program_id(0)

    @pl.when(i < n_prompt_tiles)
    def _():
        for rows in _row_groups(xp_ref.shape[0], FFN_ROW_GROUPS):
            op_ref[rows, :] = ffn(xp_ref[rows, :])

    @pl.when(i == n_prompt_tiles)
    def _():
        _store_time_major(os_ref, ffn(_time_major(xs_ref)))


def _ffn(xp, xs, ng, wg, wu, wd, fg, cast_items, *, tm, final_norm):
    n, d = xp.shape
    dff = wg.shape[1]
    n_tiles = n // tm
    cast_every = n_tiles // CAST_SLABS
    cast_in, cast_out, cast_shapes = _cast_specs(cast_items, cast_every)
    prompt_spec = pl.BlockSpec((tm, d), lambda i: (jnp.minimum(i, n_tiles - 1), 0))
    outs = pl.pallas_call(
        functools.partial(_ffn_kernel, final_norm=final_norm, n_prompt_tiles=n_tiles,
                          n_cast=len(cast_items), cast_every=cast_every),
        grid=(n_tiles + 1,),
        in_specs=[
            prompt_spec,
            _const_spec(xs.shape),
            _const_spec((1, d)),
            _const_spec((d, dff)),
            _const_spec((d, dff)),
            _const_spec((dff, d)),
            _const_spec((1, d)),
        ] + cast_in,
        out_specs=[prompt_spec, pl.BlockSpec(xs.shape, lambda i: (0, 0, 0))] + cast_out,
        out_shape=[jax.ShapeDtypeStruct((n, d), F32), jax.ShapeDtypeStruct(xs.shape, F32)] + cast_shapes,
        compiler_params=_params("arbitrary"),
        name="swiglu_ffn_final" if final_norm else "swiglu_ffn",
    )(xp, xs, ng, wg, wu, wd, fg, *[item[0] for item in cast_items])
    return outs[0], outs[1], outs[2:]


def _sgu_prompt_kernel(x_ref, ng_ref, win_ref, bin_ref, lg_ref, lb_ref, ws_ref, bs_ref,
                       wout_ref, bout_ref, *rest, tl, n_cast, cast_every):
    cast_src = rest[:n_cast]
    o_ref = rest[n_cast]
    cast_dst = rest[n_cast + 1:]
    _cast_slabs(cast_src, cast_dst, cast_every)
    c = SGU_CHUNK
    row = lax.broadcasted_iota(jnp.int32, (c, c), 0)
    col = lax.broadcasted_iota(jnp.int32, (c, c), 1)
    causal = (col <= row).astype(F32)
    w_heads = [(ws_ref[hd] * causal).astype(BF16) for hd in range(SGU_HEADS)]

    n_c = tl // c
    x = x_ref[...]
    h = _rmsnorm(x, ng_ref[...]).astype(BF16)

    def in_proj(cols):
        return _gelu_exact(_dot(h, win_ref[:, cols]) + bin_ref[:, cols])

    col_blocks = [slice(k * MXU_COLS, (k + 1) * MXU_COLS) for k in range(2 * D_MODEL // MXU_COLS)]
    half = len(col_blocks) // 2
    v_raw = jnp.concatenate([in_proj(cols) for cols in col_blocks[half:]], axis=1)
    v = _layernorm(v_raw, lg_ref[...], lb_ref[...]).astype(BF16)
    u = jnp.concatenate([in_proj(cols) for cols in col_blocks[:half]], axis=1)

    gated_cols = []
    for hd in range(SGU_HEADS):
        lanes = slice(hd * SGU_HEAD_DIM, (hd + 1) * SGU_HEAD_DIM)
        v_h = jnp.concatenate([v[i * c:(i + 1) * c, lanes] for i in range(n_c)], axis=1)
        m_h = _dot(w_heads[hd], v_h)
        bias_h = bs_ref[:, lanes]
        gated_cols.append(jnp.concatenate(
            [u[i * c:(i + 1) * c, lanes] * (m_h[:, i * SGU_HEAD_DIM:(i + 1) * SGU_HEAD_DIM] + bias_h)
             for i in range(n_c)], axis=0))
    gated = jnp.concatenate(gated_cols, axis=1).astype(BF16)
    for cols in col_blocks[:half]:
        o_ref[:, cols] = x[:, cols] + _dot(gated, wout_ref[:, cols]) + bout_ref[:, cols]


def _sgu_prompt(x, ng, win, b_in, lg, lb, ws, bs_full, wout, bout, cast_items, *, tl):
    n, d = x.shape
    cast_every = (n // tl) // CAST_SLABS
    cast_in, cast_out, cast_shapes = _cast_specs(cast_items, cast_every)
    outs = pl.pallas_call(
        functools.partial(_sgu_prompt_kernel, tl=tl, n_cast=len(cast_items), cast_every=cast_every),
        grid=(n // tl,),
        in_specs=[
            pl.BlockSpec((tl, d), lambda i: (i, 0)),
            _const_spec((1, d)),
            _const_spec((d, 2 * d)),
            _const_spec((1, 2 * d)),
            _const_spec((1, d)),
            _const_spec((1, d)),
            _const_spec(ws.shape),
            _const_spec(bs_full.shape),
            _const_spec((d, d)),
            _const_spec((1, d)),
        ] + cast_in,
        out_specs=[pl.BlockSpec((tl, d), lambda i: (i, 0))] + cast_out,
        out_shape=[jax.ShapeDtypeStruct((n, d), F32)] + cast_shapes,
        compiler_params=_params("arbitrary"),
        name="sgu_mixer_prompt",
    )(x, ng, win, b_in, lg, lb, ws, bs_full, wout, bout, *[item[0] for item in cast_items])
    return outs[0], outs[1:]


def _sgu_sample_kernel(x_ref, ng_ref, win_ref, bin_ref, lg_ref, lb_ref, w4_ref, b4_ref,
                       wout_ref, bout_ref, o_ref, v_ref):
    nb, n_t, _ = x_ref.shape
    x = _time_major(x_ref)
    h = _rmsnorm(x, ng_ref[...]).astype(BF16)
    z = _gelu_exact(_dot(h, win_ref[...]) + bin_ref[...])
    u = z[:, :D_MODEL]
    v = _layernorm(z[:, D_MODEL:], lg_ref[...], lb_ref[...])
    _store_time_major(v_ref, v)
    mixed = []
    for t in range(n_t):
        m_t = jnp.broadcast_to(b4_ref[pl.ds(t, 1), :], (nb, D_MODEL))
        for s in range(t + 1):
            m_t = m_t + w4_ref[t, pl.ds(s, 1), :] * v[s * nb:(s + 1) * nb, :]
        mixed.append(m_t)
    gated = (u * jnp.concatenate(mixed, axis=0)).astype(BF16)
    _store_time_major(o_ref, x + _dot(gated, wout_ref[...]) + bout_ref[...])


def _sgu_sample(x, ng, win, b_in, lg, lb, w4, b4, wout, bout):
    whole = lambda shape: pl.BlockSpec(shape, lambda i: (0,) * len(shape))
    return pl.pallas_call(
        _sgu_sample_kernel,
        grid=(1,),
        in_specs=[whole(a.shape) for a in (x, ng, win, b_in, lg, lb, w4, b4, wout, bout)],
        out_specs=[whole(x.shape), whole(x.shape)],
        out_shape=[jax.ShapeDtypeStruct(x.shape, F32), jax.ShapeDtypeStruct(x.shape, F32)],
        compiler_params=_params("arbitrary"),
        name="sgu_mixer_sample",
    )(x, ng, win, b_in, lg, lb, w4, b4, wout, bout)


def kernel(x_prompt, x_sample, state_conv, conv_norm_g, conv_w_pw1, conv_b_pw1, conv_w_dw, conv_b_dw, conv_ln_g, conv_ln_b, conv_w_pw2, conv_b_pw2, sgu_norm_g, sgu_w_in, sgu_b_in, sgu_ln_g, sgu_ln_b, sgu_w_s, sgu_b_s, sgu_w_out, sgu_b_out, ffn_norm_g, ffn_w_gate, ffn_w_up, ffn_w_down, final_norm_g):
    batch, seq, d = x_prompt.shape
    n_t = x_sample.shape[1]
    row = lambda a: a.reshape(1, -1)
    ffn_items = lambda i: [(w, i) for w in (ffn_w_gate, ffn_w_up, ffn_w_down)]

    xp = x_prompt.reshape(batch * seq, d)
    st_t = jnp.transpose(state_conv[0], (1, 0, 2))

    w1 = conv_w_pw1[0].astype(BF16)
    w2 = conv_w_pw2[0].astype(BF16)
    wdw = conv_w_dw[0]
    wdw_c = jnp.pad(wdw, ((0, CTX_PAD - CONV_WIDTH), (0, 0))).reshape(CTX_PAD, N_LANE_CHUNKS, LANES)
    wdw_c = jnp.transpose(wdw_c, (1, 0, 2))
    bdw_c = conv_b_dw[0].reshape(N_LANE_CHUNKS, 1, LANES)
    conv_args = (row(conv_norm_g[0]), w1, row(conv_b_pw1[0]))
    conv_tail = (row(conv_ln_g[0]), row(conv_ln_b[0]), w2, row(conv_b_pw2[0]))

    xp, conv_p, ffn0_bf16 = _conv_prompt(xp, *conv_args, wdw_c, bdw_c, *conv_tail, ffn_items(0),
                                         batch=batch, seq=seq, tl=TOKEN_TILE)
    xs, conv_s_t = _conv_sample(x_sample, st_t, *conv_args, wdw, row(conv_b_dw[0]), *conv_tail, bc=32)

    fg = row(final_norm_g)
    xp, xs, (win, wout) = _ffn(xp, xs, row(ffn_norm_g[0]), *ffn0_bf16, fg, [(sgu_w_in, 0), (sgu_w_out, 0)],
                               tm=FFN_TOKEN_TILE, final_norm=False)

    sgu_args = (row(sgu_norm_g[0]), win, row(sgu_b_in[0]), row(sgu_ln_g[0]), row(sgu_ln_b[0]))
    bs_full = jnp.repeat(jnp.transpose(sgu_b_s[0]), SGU_HEAD_DIM, axis=1)
    w4 = jnp.repeat(jnp.transpose(sgu_w_s[0][:, :n_t, :n_t], (1, 2, 0)), SGU_HEAD_DIM, axis=2)
    b4 = bs_full[:n_t]

    xp, ffn1_bf16 = _sgu_prompt(xp, *sgu_args, sgu_w_s[0], bs_full, wout, row(sgu_b_out[0]), ffn_items(1),
                                tl=TOKEN_TILE)
    xs, v_s = _sgu_sample(xs, *sgu_args, w4, b4, wout, row(sgu_b_out[0]))

    yp, ys, _ = _ffn(xp, xs, row(ffn_norm_g[1]), *ffn1_bf16, fg, [], tm=FFN_TOKEN_TILE, final_norm=True)

    new_conv_sample = jnp.transpose(conv_s_t, (1, 0, 2))[None]
    return (yp.reshape(batch, seq, d), ys, conv_p[None], new_conv_sample, v_s[None])
```

```python
import functools

import jax
import jax.numpy as jnp
from jax import lax
from jax.experimental import pallas as pl
from jax.experimental.pallas import tpu as pltpu

D_MODEL = 1024
CONV_WIDTH = 31
CTX = CONV_WIDTH - 1
SGU_HEADS = 8
SGU_HEAD_DIM = D_MODEL // SGU_HEADS
SGU_CHUNK = 128
RMS_EPS = 1e-6
LN_EPS = 1e-5

LANES = 128
MXU_COLS = 256
BF16_SUBLANES = 16
N_LANE_CHUNKS = D_MODEL // LANES
CTX_PAD = 32
VMEM_LIMIT = 56 * 1024 * 1024
TOKEN_TILE = 1024
SAMPLE_BATCH_TILE = 32
FFN0_ROW_GROUPS = 4
FFN1_ROW_GROUPS = 2
FFN_TOKEN_TILE = 1024
CAST_SLABS = 8

F32 = jnp.float32
BF16 = jnp.bfloat16


def _rmsnorm(x, g):
    return x * lax.rsqrt(jnp.mean(x * x, axis=-1, keepdims=True) + RMS_EPS) * g


def _layernorm(x, g, b):
    mu = jnp.mean(x, axis=-1, keepdims=True)
    xc = x - mu
    var = jnp.mean(xc * xc, axis=-1, keepdims=True)
    return xc * lax.rsqrt(var + LN_EPS) * g + b


def _sigmoid(x):
    return 1.0 / (1.0 + jnp.exp(-x))


def _silu(x):
    return x * _sigmoid(x)


def _gelu_exact(x):
    return 0.5 * x * (1.0 + lax.erf(x * (2.0 ** -0.5)))


def _dot(a, b):
    return jnp.dot(a, b, preferred_element_type=F32)


def _const_spec(shape):
    nd = len(shape)
    return pl.BlockSpec(shape, lambda *_: (0,) * nd, pipeline_mode=pl.Buffered(1))


def _params(*sem):
    return pltpu.CompilerParams(dimension_semantics=sem, vmem_limit_bytes=VMEM_LIMIT)


def _row_groups(n_rows, n_groups=1):
    size = n_rows // n_groups
    return [pl.ds(g * size, size) for g in range(n_groups)]


def _time_major(x_ref):
    return jnp.concatenate([x_ref[:, t, :] for t in range(x_ref.shape[1])], axis=0)


def _store_time_major(o_ref, y):
    nb, n_t, _ = o_ref.shape
    for t in range(n_t):
        o_ref[:, t, :] = y[t * nb:(t + 1) * nb, :]


def _cast_specs(items, every):
    slab_of = lambda s: jnp.minimum(s // every, CAST_SLABS - 1)
    in_specs, out_specs, shapes = [], [], []
    for stacked, layer in items:
        _, rows, cols = stacked.shape
        assert rows % (CAST_SLABS * BF16_SUBLANES) == 0
        slab = rows // CAST_SLABS
        in_specs.append(pl.BlockSpec((None, slab, cols), lambda s, layer=layer: (layer, slab_of(s), 0)))
        out_specs.append(pl.BlockSpec((slab, cols), lambda s: (slab_of(s), 0)))
        shapes.append(jax.ShapeDtypeStruct((rows, cols), BF16))
    return in_specs, out_specs, shapes


def _cast_slabs(src_refs, dst_refs, every):
    if not src_refs:
        return

    @pl.when(pl.program_id(0) % every == 0)
    def _():
        for s, d in zip(src_refs, dst_refs):
            d[...] = s[...].astype(BF16)


def _conv_prompt_kernel(x_ref, ng_ref, w1_ref, b1_ref, wdw_ref, bdw_ref, lg_ref, lb_ref,
                        w2_ref, b2_ref, *rest, tl, tiles_per_seq, n_cast, cast_every):
    cast_src = rest[:n_cast]
    o_ref, ctx_ref = rest[n_cast:n_cast + 2]
    cast_dst = rest[n_cast + 2:2 * n_cast + 2]
    fbuf, cbuf = rest[2 * n_cast + 2:]
    _cast_slabs(cast_src, cast_dst, cast_every)
    l = pl.program_id(0) % tiles_per_seq

    @pl.when(l == 0)
    def _():
        fbuf[:, pl.ds(0, CTX_PAD), :] = jnp.zeros((N_LANE_CHUNKS, CTX_PAD, LANES), F32)

    @pl.when(l > 0)
    def _():
        fbuf[:, pl.ds(0, CTX_PAD), :] = fbuf[:, pl.ds(tl, CTX_PAD), :]

    h = _rmsnorm(x_ref[...], ng_ref[...]).astype(BF16)
    for j in range(N_LANE_CHUNKS):
        value_cols = slice(j * LANES, (j + 1) * LANES)
        gate_cols = slice(D_MODEL + j * LANES, D_MODEL + (j + 1) * LANES)
        w_pair = jnp.concatenate([w1_ref[:, value_cols], w1_ref[:, gate_cols]], axis=1)
        b_pair = jnp.concatenate([b1_ref[:, value_cols], b1_ref[:, gate_cols]], axis=1)
        a = _dot(h, w_pair) + b_pair
        fbuf[j, pl.ds(CTX_PAD, tl), :] = a[:, :LANES] * _sigmoid(a[:, LANES:])

    shift0 = CTX_PAD - CTX
    row_chunk = 64

    def lane_chunk(j, carry):
        bias = bdw_ref[j]
        for r0 in range(0, tl, row_chunk):
            acc = jnp.broadcast_to(bias, (row_chunk, LANES))
            for k in range(CONV_WIDTH):
                w_k = wdw_ref[j, pl.ds(k, 1), :]
                acc = acc + w_k * fbuf[j, pl.ds(r0 + k + shift0, row_chunk), :]
            cbuf[j, pl.ds(r0, row_chunk), :] = acc
        return carry

    lax.fori_loop(0, N_LANE_CHUNKS, lane_chunk, 0)

    conv = jnp.concatenate([cbuf[j] for j in range(N_LANE_CHUNKS)], axis=1)
    y = _silu(_layernorm(conv, lg_ref[...], lb_ref[...])).astype(BF16)
    for k in range(D_MODEL // MXU_COLS):
        cols = slice(k * MXU_COLS, (k + 1) * MXU_COLS)
        o_ref[:, cols] = x_ref[:, cols] + _dot(y, w2_ref[:, cols]) + b2_ref[:, cols]

    @pl.when(l == tiles_per_seq - 1)
    def _():
        for j in range(N_LANE_CHUNKS):
            ctx_ref[0, :, j * LANES:(j + 1) * LANES] = fbuf[j, pl.ds(tl + CTX_PAD - CTX, CTX), :]


def _conv_prompt(x, ng, w1, b1, wdw_c, bdw_c, lg, lb, w2, b2, cast_items, *, batch, seq, tl):
    tiles_per_seq = seq // tl
    n_tiles = batch * tiles_per_seq
    d = D_MODEL
    cast_every = n_tiles // CAST_SLABS
    cast_in, cast_out, cast_shapes = _cast_specs(cast_items, cast_every)
    outs = pl.pallas_call(
        functools.partial(_conv_prompt_kernel, tl=tl, tiles_per_seq=tiles_per_seq, n_cast=len(cast_items),
                          cast_every=cast_every),
        grid=(n_tiles,),
        in_specs=[
            pl.BlockSpec((tl, d), lambda s: (s, 0)),
            _const_spec((1, d)),
            _const_spec(w1.shape),
            _const_spec(b1.shape),
            _const_spec(wdw_c.shape),
            _const_spec(bdw_c.shape),
            _const_spec((1, d)),
            _const_spec((1, d)),
            _const_spec((d, d)),
            _const_spec((1, d)),
        ] + cast_in,
        out_specs=[
            pl.BlockSpec((tl, d), lambda s: (s, 0)),
            pl.BlockSpec((1, CTX, d), lambda s: (s // tiles_per_seq, 0, 0)),
        ] + cast_out,
        out_shape=[
            jax.ShapeDtypeStruct((batch * seq, d), F32),
            jax.ShapeDtypeStruct((batch, CTX, d), F32),
        ] + cast_shapes,
        scratch_shapes=[
            pltpu.VMEM((N_LANE_CHUNKS, tl + CTX_PAD, LANES), F32),
            pltpu.VMEM((N_LANE_CHUNKS, tl, LANES), F32),
        ],
        compiler_params=_params("arbitrary"),
        name="conv_mixer_prompt",
    )(x, ng, w1, b1, wdw_c, bdw_c, lg, lb, w2, b2, *[item[0] for item in cast_items])
    return outs[0], outs[1], outs[2:]


def _conv_sample_kernel(x_ref, st_ref, ng_ref, w1_ref, b1_ref, wdw_ref, bdw_ref, lg_ref, lb_ref,
                        w2_ref, b2_ref, o_ref, nst_ref, gbuf, cbuf, *, n_t, bc):
    x = _time_major(x_ref)
    h = _rmsnorm(x, ng_ref[...]).astype(BF16)
    a = _dot(h, w1_ref[...]) + b1_ref[...]
    glu = a[:, :D_MODEL] * _sigmoid(a[:, D_MODEL:])
    for t in range(n_t):
        gbuf[t] = glu[t * bc:(t + 1) * bc, :]

    nst_ref[pl.ds(0, CTX - n_t)] = st_ref[pl.ds(n_t, CTX - n_t)]
    nst_ref[pl.ds(CTX - n_t, n_t)] = gbuf[...]

    def lane_chunk(j, carry):
        lanes = pl.ds(pl.multiple_of(j * LANES, LANES), LANES)
        bias = bdw_ref[:, lanes]
        accs = [jnp.broadcast_to(bias, (bc, LANES)) for _ in range(n_t)]
        for m in range(CTX + n_t):
            f = st_ref[m, :, lanes] if m < CTX else gbuf[m - CTX, :, lanes]
            for t in range(n_t):
                k = m - t
                if 0 <= k < CONV_WIDTH:
                    accs[t] = accs[t] + wdw_ref[pl.ds(k, 1), lanes] * f
        for t in range(n_t):
            cbuf[t, :, lanes] = accs[t]
        return carry

    lax.fori_loop(0, N_LANE_CHUNKS, lane_chunk, 0)

    conv = jnp.concatenate([cbuf[t] for t in range(n_t)], axis=0)
    y = _silu(_layernorm(conv, lg_ref[...], lb_ref[...])).astype(BF16)
    _store_time_major(o_ref, x + _dot(y, w2_ref[...]) + b2_ref[...])


def _conv_sample(x, st_t, ng, w1, b1, wdw, bdw, lg, lb, w2, b2, *, bc):
    nb, n_t, d = x.shape
    return pl.pallas_call(
        functools.partial(_conv_sample_kernel, n_t=n_t, bc=bc),
        grid=(nb // bc,),
        in_specs=[
            pl.BlockSpec((bc, n_t, d), lambda i: (i, 0, 0)),
            pl.BlockSpec((CTX, bc, d), lambda i: (0, i, 0)),
            _const_spec((1, d)),
            _const_spec((d, 2 * d)),
            _const_spec((1, 2 * d)),
            _const_spec(wdw.shape),
            _const_spec((1, d)),
            _const_spec((1, d)),
            _const_spec((1, d)),
            _const_spec((d, d)),
            _const_spec((1, d)),
        ],
        out_specs=[
            pl.BlockSpec((bc, n_t, d), lambda i: (i, 0, 0)),
            pl.BlockSpec((CTX, bc, d), lambda i: (0, i, 0)),
        ],
        out_shape=[
            jax.ShapeDtypeStruct((nb, n_t, d), F32),
            jax.ShapeDtypeStruct((CTX, nb, d), F32),
        ],
        scratch_shapes=[
            pltpu.VMEM((n_t, bc, d), F32),
            pltpu.VMEM((n_t, bc, d), F32),
        ],
        compiler_params=_params("arbitrary"),
        name="conv_mixer_sample",
    )(x, st_t, ng, w1, b1, wdw, bdw, lg, lb, w2, b2)


def _ffn_kernel(xp_ref, xs_ref, ng_ref, wg_ref, wu_ref, wd_ref, fg_ref, *rest,
                final_norm, row_groups, n_prompt_tiles, n_cast, cast_every):
    cast_src = rest[:n_cast]
    op_ref, os_ref = rest[n_cast:n_cast + 2]
    cast_dst = rest[n_cast + 2:]
    _cast_slabs(cast_src, cast_dst, cast_every)

    def ffn(x):
        h = _rmsnorm(x, ng_ref[...]).astype(BF16)
        acts = []
        for k in range(wg_ref.shape[1] // MXU_COLS):
            cols = slice(k * MXU_COLS, (k + 1) * MXU_COLS)
            acts.append((_silu(_dot(h, wg_ref[:, cols])) * _dot(h, wu_ref[:, cols])).astype(BF16))
        y = x + _dot(jnp.concatenate(acts, axis=1), wd_ref[...])
        if final_norm:
            y = _rmsnorm(y, fg_ref[...])
        return y

    i = pl.program_id(0)

    @pl.when(i < n_prompt_tiles)
    def _():
        for rows in _row_groups(xp_ref.shape[0], row_groups):
            op_ref[rows, :] = ffn(xp_ref[rows, :])

    @pl.when(i == n_prompt_tiles)
    def _():
        _store_time_major(os_ref, ffn(_time_major(xs_ref)))


def _ffn(xp, xs, ng, wg, wu, wd, fg, cast_items, *, tm, row_groups, final_norm):
    n, d = xp.shape
    dff = wg.shape[1]
    n_tiles = n // tm
    cast_every = n_tiles // CAST_SLABS
    cast_in, cast_out, cast_shapes = _cast_specs(cast_items, cast_every)
    prompt_spec = pl.BlockSpec((tm, d), lambda i: (jnp.minimum(i, n_tiles - 1), 0))
    outs = pl.pallas_call(
        functools.partial(_ffn_kernel, final_norm=final_norm, row_groups=row_groups, n_prompt_tiles=n_tiles,
                          n_cast=len(cast_items), cast_every=cast_every),
        grid=(n_tiles + 1,),
        in_specs=[
            prompt_spec,
            _const_spec(xs.shape),
            _const_spec((1, d)),
            _const_spec((d, dff)),
            _const_spec((d, dff)),
            _const_spec((dff, d)),
            _const_spec((1, d)),
        ] + cast_in,
        out_specs=[prompt_spec, pl.BlockSpec(xs.shape, lambda i: (0, 0, 0))] + cast_out,
        out_shape=[jax.ShapeDtypeStruct((n, d), F32), jax.ShapeDtypeStruct(xs.shape, F32)] + cast_shapes,
        compiler_params=_params("arbitrary"),
        name="swiglu_ffn_final" if final_norm else "swiglu_ffn",
    )(xp, xs, ng, wg, wu, wd, fg, *[item[0] for item in cast_items])
    return outs[0], outs[1], outs[2:]


def _sgu_prompt_kernel(x_ref, ng_ref, win_ref, bin_ref, lg_ref, lb_ref, ws_ref, bs_ref,
                       wout_ref, bout_ref, *rest, tl, n_cast, cast_every):
    cast_src = rest[:n_cast]
    o_ref = rest[n_cast]
    cast_dst = rest[n_cast + 1:]
    _cast_slabs(cast_src, cast_dst, cast_every)
    c = SGU_CHUNK
    row = lax.broadcasted_iota(jnp.int32, (c, c), 0)
    col = lax.broadcasted_iota(jnp.int32, (c, c), 1)
    causal = (col <= row).astype(F32)
    w_heads = [(ws_ref[hd] * causal).astype(BF16) for hd in range(SGU_HEADS)]

    n_c = tl // c
    x = x_ref[...]
    h = _rmsnorm(x, ng_ref[...]).astype(BF16)

    def in_proj(cols):
        return _gelu_exact(_dot(h, win_ref[:, cols]) + bin_ref[:, cols])

    col_blocks = [slice(k * MXU_COLS, (k + 1) * MXU_COLS) for k in range(2 * D_MODEL // MXU_COLS)]
    half = len(col_blocks) // 2
    v_raw = jnp.concatenate([in_proj(cols) for cols in col_blocks[half:]], axis=1)
    v = _layernorm(v_raw, lg_ref[...], lb_ref[...]).astype(BF16)
    u = jnp.concatenate([in_proj(cols) for cols in col_blocks[:half]], axis=1)

    gated_cols = []
    for hd in range(SGU_HEADS):
        lanes = slice(hd * SGU_HEAD_DIM, (hd + 1) * SGU_HEAD_DIM)
        v_h = jnp.concatenate([v[i * c:(i + 1) * c, lanes] for i in range(n_c)], axis=1)
        m_h = _dot(w_heads[hd], v_h)
        bias_h = bs_ref[:, lanes]
        gated_cols.append(jnp.concatenate(
            [u[i * c:(i + 1) * c, lanes] * (m_h[:, i * SGU_HEAD_DIM:(i + 1) * SGU_HEAD_DIM] + bias_h)
             for i in range(n_c)], axis=0))
    gated = jnp.concatenate(gated_cols, axis=1).astype(BF16)
    for cols in col_blocks[:half]:
        o_ref[:, cols] = x[:, cols] + _dot(gated, wout_ref[:, cols]) + bout_ref[:, cols]


def _sgu_prompt(x, ng, win, b_in, lg, lb, ws, bs_full, wout, bout, cast_items, *, tl):
    n, d = x.shape
    cast_every = (n // tl) // CAST_SLABS
    cast_in, cast_out, cast_shapes = _cast_specs(cast_items, cast_every)
    outs = pl.pallas_call(
        functools.partial(_sgu_prompt_kernel, tl=tl, n_cast=len(cast_items), cast_every=cast_every),
        grid=(n // tl,),
        in_specs=[
            pl.BlockSpec((tl, d), lambda i: (i, 0)),
            _const_spec((1, d)),
            _const_spec((d, 2 * d)),
            _const_spec((1, 2 * d)),
            _const_spec((1, d)),
            _const_spec((1, d)),
            _const_spec(ws.shape),
            _const_spec(bs_full.shape),
            _const_spec((d, d)),
            _const_spec((1, d)),
        ] + cast_in,
        out_specs=[pl.BlockSpec((tl, d), lambda i: (i, 0))] + cast_out,
        out_shape=[jax.ShapeDtypeStruct((n, d), F32)] + cast_shapes,
        compiler_params=_params("arbitrary"),
        name="sgu_mixer_prompt",
    )(x, ng, win, b_in, lg, lb, ws, bs_full, wout, bout, *[item[0] for item in cast_items])
    return outs[0], outs[1:]


def _sgu_sample_kernel(x_ref, ng_ref, win_ref, bin_ref, lg_ref, lb_ref, w4_ref, b4_ref,
                       wout_ref, bout_ref, o_ref, v_ref):
    nb, n_t, _ = x_ref.shape
    x = _time_major(x_ref)
    h = _rmsnorm(x, ng_ref[...]).astype(BF16)
    z = _gelu_exact(_dot(h, win_ref[...]) + bin_ref[...])
    u = z[:, :D_MODEL]
    v = _layernorm(z[:, D_MODEL:], lg_ref[...], lb_ref[...])
    _store_time_major(v_ref, v)
    mixed = []
    for t in range(n_t):
        m_t = jnp.broadcast_to(b4_ref[pl.ds(t, 1), :], (nb, D_MODEL))
        for s in range(t + 1):
            m_t = m_t + w4_ref[t, pl.ds(s, 1), :] * v[s * nb:(s + 1) * nb, :]
        mixed.append(m_t)
    gated = (u * jnp.concatenate(mixed, axis=0)).astype(BF16)
    _store_time_major(o_ref, x + _dot(gated, wout_ref[...]) + bout_ref[...])


def _sgu_sample(x, ng, win, b_in, lg, lb, w4, b4, wout, bout):
    whole = lambda shape: pl.BlockSpec(shape, lambda i: (0,) * len(shape))
    return pl.pallas_call(
        _sgu_sample_kernel,
        grid=(1,),
        in_specs=[whole(a.shape) for a in (x, ng, win, b_in, lg, lb, w4, b4, wout, bout)],
        out_specs=[whole(x.shape), whole(x.shape)],
        out_shape=[jax.ShapeDtypeStruct(x.shape, F32), jax.ShapeDtypeStruct(x.shape, F32)],
        compiler_params=_params("arbitrary"),
        name="sgu_mixer_sample",
    )(x, ng, win, b_in, lg, lb, w4, b4, wout, bout)


def kernel(x_prompt, x_sample, state_conv, conv_norm_g, conv_w_pw1, conv_b_pw1, conv_w_dw, conv_b_dw, conv_ln_g, conv_ln_b, conv_w_pw2, conv_b_pw2, sgu_norm_g, sgu_w_in, sgu_b_in, sgu_ln_g, sgu_ln_b, sgu_w_s, sgu_b_s, sgu_w_out, sgu_b_out, ffn_norm_g, ffn_w_gate, ffn_w_up, ffn_w_down, final_norm_g):
    batch, seq, d = x_prompt.shape
    n_t = x_sample.shape[1]
    row = lambda a: a.reshape(1, -1)
    ffn_items = lambda i: [(w, i) for w in (ffn_w_gate, ffn_w_up, ffn_w_down)]

    xp = x_prompt.reshape(batch * seq, d)
    st_t = jnp.transpose(state_conv[0], (1, 0, 2))

    w1 = conv_w_pw1[0].astype(BF16)
    w2 = conv_w_pw2[0].astype(BF16)
    wdw = conv_w_dw[0]
    wdw_c = jnp.pad(wdw, ((0, CTX_PAD - CONV_WIDTH), (0, 0))).reshape(CTX_PAD, N_LANE_CHUNKS, LANES)
    wdw_c = jnp.transpose(wdw_c, (1, 0, 2))
    bdw_c = conv_b_dw[0].reshape(N_LANE_CHUNKS, 1, LANES)
    conv_args = (row(conv_norm_g[0]), w1, row(conv_b_pw1[0]))
    conv_tail = (row(conv_ln_g[0]), row(conv_ln_b[0]), w2, row(conv_b_pw2[0]))

    xp, conv_p, ffn0_bf16 = _conv_prompt(xp, *conv_args, wdw_c, bdw_c, *conv_tail, ffn_items(0),
                                         batch=batch, seq=seq, tl=TOKEN_TILE)
    xs, conv_s_t = _conv_sample(x_sample, st_t, *conv_args, wdw, row(conv_b_dw[0]), *conv_tail,
                                bc=SAMPLE_BATCH_TILE)

    fg = row(final_norm_g)
    xp, xs, (win, wout) = _ffn(xp, xs, row(ffn_norm_g[0]), *ffn0_bf16, fg, [(sgu_w_in, 0), (sgu_w_out, 0)],
                               tm=FFN_TOKEN_TILE, row_groups=FFN0_ROW_GROUPS, final_norm=False)

    sgu_args = (row(sgu_norm_g[0]), win, row(sgu_b_in[0]), row(sgu_ln_g[0]), row(sgu_ln_b[0]))
    bs_full = jnp.repeat(jnp.transpose(sgu_b_s[0]), SGU_HEAD_DIM, axis=1)
    w4 = jnp.repeat(jnp.transpose(sgu_w_s[0][:, :n_t, :n_t], (1, 2, 0)), SGU_HEAD_DIM, axis=2)
    b4 = bs_full[:n_t]

    xp, ffn1_bf16 = _sgu_prompt(xp, *sgu_args, sgu_w_s[0], bs_full, wout, row(sgu_b_out[0]), ffn_items(1),
                                tl=TOKEN_TILE)
    xs, v_s = _sgu_sample(xs, *sgu_args, w4, b4, wout, row(sgu_b_out[0]))

    yp, ys, _ = _ffn(xp, xs, row(ffn_norm_g[1]), *ffn1_bf16, fg, [],
                     tm=FFN_TOKEN_TILE, row_groups=FFN1_ROW_GROUPS, final_norm=True)

    new_conv_sample = jnp.transpose(conv_s_t, (1, 0, 2))[None]
    return (yp.reshape(batch, seq, d), ys, conv_p[None], new_conv_sample, v_s[None])
```

```python
import functools

import jax
import jax.numpy as jnp
from jax import lax
from jax.experimental import pallas as pl
from jax.experimental.pallas import tpu as pltpu

D_MODEL = 1024
CONV_WIDTH = 31
CTX = CONV_WIDTH - 1
SGU_HEADS = 8
SGU_HEAD_DIM = D_MODEL // SGU_HEADS
SGU_CHUNK = 128
RMS_EPS = 1e-6
LN_EPS = 1e-5
NEG_LOG2_E = -1.4426950408889634

LANES = 128
MXU_COLS = 256
BF16_SUBLANES = 16
N_LANE_CHUNKS = D_MODEL // LANES
CTX_PAD = 32
VMEM_LIMIT = 56 * 1024 * 1024
TOKEN_TILE = 1024
SAMPLE_BATCH_TILE = 32
FFN0_ROW_GROUPS = 4
FFN1_ROW_GROUPS = 2
FFN_TOKEN_TILE = 1024
CAST_SLABS = 8

F32 = jnp.float32
BF16 = jnp.bfloat16


def _rmsnorm(x, g):
    return x * lax.rsqrt(jnp.mean(x * x, axis=-1, keepdims=True) + RMS_EPS) * g


def _layernorm(x, g, b):
    mu = jnp.mean(x, axis=-1, keepdims=True)
    xc = x - mu
    var = jnp.mean(xc * xc, axis=-1, keepdims=True)
    return xc * lax.rsqrt(var + LN_EPS) * g + b


def _sigmoid(x):
    return 1.0 / (1.0 + jnp.exp2(x * NEG_LOG2_E))


def _silu(x):
    return x * _sigmoid(x)


def _gelu_exact(x):
    return 0.5 * x * (1.0 + lax.erf(x * (2.0 ** -0.5)))


def _dot(a, b):
    return jnp.dot(a, b, preferred_element_type=F32)


def _const_spec(shape):
    nd = len(shape)
    return pl.BlockSpec(shape, lambda *_: (0,) * nd, pipeline_mode=pl.Buffered(1))


def _params(*sem):
    return pltpu.CompilerParams(dimension_semantics=sem, vmem_limit_bytes=VMEM_LIMIT)


def _row_groups(n_rows, n_groups=1):
    size = n_rows // n_groups
    return [pl.ds(g * size, size) for g in range(n_groups)]


def _time_major(x_ref):
    return jnp.concatenate([x_ref[:, t, :] for t in range(x_ref.shape[1])], axis=0)


def _store_time_major(o_ref, y):
    nb, n_t, _ = o_ref.shape
    for t in range(n_t):
        o_ref[:, t, :] = y[t * nb:(t + 1) * nb, :]


def _cast_specs(items, every):
    slab_of = lambda s: jnp.minimum(s // every, CAST_SLABS - 1)
    in_specs, out_specs, shapes = [], [], []
    for stacked, layer in items:
        _, rows, cols = stacked.shape
        assert rows % (CAST_SLABS * BF16_SUBLANES) == 0
        slab = rows // CAST_SLABS
        in_specs.append(pl.BlockSpec((None, slab, cols), lambda s, layer=layer: (layer, slab_of(s), 0)))
        out_specs.append(pl.BlockSpec((slab, cols), lambda s: (slab_of(s), 0)))
        shapes.append(jax.ShapeDtypeStruct((rows, cols), BF16))
    return in_specs, out_specs, shapes


def _cast_slabs(src_refs, dst_refs, every):
    if not src_refs:
        return

    @pl.when(pl.program_id(0) % every == 0)
    def _():
        for s, d in zip(src_refs, dst_refs):
            d[...] = s[...].astype(BF16)


def _conv_prompt_kernel(x_ref, ng_ref, w1_ref, b1_ref, wdw_ref, bdw_ref, lg_ref, lb_ref,
                        w2_ref, b2_ref, *rest, tl, tiles_per_seq, n_cast, cast_every):
    cast_src = rest[:n_cast]
    o_ref, ctx_ref = rest[n_cast:n_cast + 2]
    cast_dst = rest[n_cast + 2:2 * n_cast + 2]
    fbuf, cbuf = rest[2 * n_cast + 2:]
    _cast_slabs(cast_src, cast_dst, cast_every)
    l = pl.program_id(0) % tiles_per_seq

    @pl.when(l == 0)
    def _():
        fbuf[:, pl.ds(0, CTX_PAD), :] = jnp.zeros((N_LANE_CHUNKS, CTX_PAD, LANES), F32)

    @pl.when(l > 0)
    def _():
        fbuf[:, pl.ds(0, CTX_PAD), :] = fbuf[:, pl.ds(tl, CTX_PAD), :]

    h = _rmsnorm(x_ref[...], ng_ref[...]).astype(BF16)
    for j in range(N_LANE_CHUNKS):
        value_cols = slice(j * LANES, (j + 1) * LANES)
        gate_cols = slice(D_MODEL + j * LANES, D_MODEL + (j + 1) * LANES)
        w_pair = jnp.concatenate([w1_ref[:, value_cols], w1_ref[:, gate_cols]], axis=1)
        b_pair = jnp.concatenate([b1_ref[:, value_cols], b1_ref[:, gate_cols]], axis=1)
        a = _dot(h, w_pair) + b_pair
        fbuf[j, pl.ds(CTX_PAD, tl), :] = a[:, :LANES] * _sigmoid(a[:, LANES:])

    shift0 = CTX_PAD - CTX
    row_chunk = 64

    def lane_chunk(j, carry):
        bias = bdw_ref[j]
        for r0 in range(0, tl, row_chunk):
            acc = jnp.broadcast_to(bias, (row_chunk, LANES))
            for k in range(CONV_WIDTH):
                w_k = wdw_ref[j, pl.ds(k, 1), :]
                acc = acc + w_k * fbuf[j, pl.ds(r0 + k + shift0, row_chunk), :]
            cbuf[j, pl.ds(r0, row_chunk), :] = acc
        return carry

    lax.fori_loop(0, N_LANE_CHUNKS, lane_chunk, 0)

    conv = jnp.concatenate([cbuf[j] for j in range(N_LANE_CHUNKS)], axis=1)
    y = _silu(_layernorm(conv, lg_ref[...], lb_ref[...])).astype(BF16)
    for k in range(D_MODEL // MXU_COLS):
        cols = slice(k * MXU_COLS, (k + 1) * MXU_COLS)
        o_ref[:, cols] = x_ref[:, cols] + _dot(y, w2_ref[:, cols]) + b2_ref[:, cols]

    @pl.when(l == tiles_per_seq - 1)
    def _():
        for j in range(N_LANE_CHUNKS):
            ctx_ref[0, :, j * LANES:(j + 1) * LANES] = fbuf[j, pl.ds(tl + CTX_PAD - CTX, CTX), :]


def _conv_prompt(x, ng, w1, b1, wdw_c, bdw_c, lg, lb, w2, b2, cast_items, *, batch, seq, tl):
    tiles_per_seq = seq // tl
    n_tiles = batch * tiles_per_seq
    d = D_MODEL
    cast_every = n_tiles // CAST_SLABS
    cast_in, cast_out, cast_shapes = _cast_specs(cast_items, cast_every)
    outs = pl.pallas_call(
        functools.partial(_conv_prompt_kernel, tl=tl, tiles_per_seq=tiles_per_seq, n_cast=len(cast_items),
                          cast_every=cast_every),
        grid=(n_tiles,),
        in_specs=[
            pl.BlockSpec((tl, d), lambda s: (s, 0)),
            _const_spec((1, d)),
            _const_spec(w1.shape),
            _const_spec(b1.shape),
            _const_spec(wdw_c.shape),
            _const_spec(bdw_c.shape),
            _const_spec((1, d)),
            _const_spec((1, d)),
            _const_spec((d, d)),
            _const_spec((1, d)),
        ] + cast_in,
        out_specs=[
            pl.BlockSpec((tl, d), lambda s: (s, 0)),
            pl.BlockSpec((1, CTX, d), lambda s: (s // tiles_per_seq, 0, 0)),
        ] + cast_out,
        out_shape=[
            jax.ShapeDtypeStruct((batch * seq, d), F32),
            jax.ShapeDtypeStruct((batch, CTX, d), F32),
        ] + cast_shapes,
        scratch_shapes=[
            pltpu.VMEM((N_LANE_CHUNKS, tl + CTX_PAD, LANES), F32),
            pltpu.VMEM((N_LANE_CHUNKS, tl, LANES), F32),
        ],
        compiler_params=_params("arbitrary"),
        name="conv_mixer_prompt",
    )(x, ng, w1, b1, wdw_c, bdw_c, lg, lb, w2, b2, *[item[0] for item in cast_items])
    return outs[0], outs[1], outs[2:]


def _conv_sample_kernel(x_ref, st_ref, ng_ref, w1_ref, b1_ref, wdw_ref, bdw_ref, lg_ref, lb_ref,
                        w2_ref, b2_ref, o_ref, nst_ref, gbuf, cbuf, *, n_t, bc):
    x = _time_major(x_ref)
    h = _rmsnorm(x, ng_ref[...]).astype(BF16)
    a = _dot(h, w1_ref[...]) + b1_ref[...]
    glu = a[:, :D_MODEL] * _sigmoid(a[:, D_MODEL:])
    for t in range(n_t):
        gbuf[t] = glu[t * bc:(t + 1) * bc, :]

    nst_ref[pl.ds(0, CTX - n_t)] = st_ref[pl.ds(n_t, CTX - n_t)]
    nst_ref[pl.ds(CTX - n_t, n_t)] = gbuf[...]

    def lane_chunk(j, carry):
        lanes = pl.ds(pl.multiple_of(j * LANES, LANES), LANES)
        bias = bdw_ref[:, lanes]
        accs = [jnp.broadcast_to(bias, (bc, LANES)) for _ in range(n_t)]
        for m in range(CTX + n_t):
            f = st_ref[m, :, lanes] if m < CTX else gbuf[m - CTX, :, lanes]
            for t in range(n_t):
                k = m - t
                if 0 <= k < CONV_WIDTH:
                    accs[t] = accs[t] + wdw_ref[pl.ds(k, 1), lanes] * f
        for t in range(n_t):
            cbuf[t, :, lanes] = accs[t]
        return carry

    lax.fori_loop(0, N_LANE_CHUNKS, lane_chunk, 0)

    conv = jnp.concatenate([cbuf[t] for t in range(n_t)], axis=0)
    y = _silu(_layernorm(conv, lg_ref[...], lb_ref[...])).astype(BF16)
    _store_time_major(o_ref, x + _dot(y, w2_ref[...]) + b2_ref[...])


def _conv_sample(x, st_t, ng, w1, b1, wdw, bdw, lg, lb, w2, b2, *, bc):
    nb, n_t, d = x.shape
    return pl.pallas_call(
        functools.partial(_conv_sample_kernel, n_t=n_t, bc=bc),
        grid=(nb // bc,),
        in_specs=[
            pl.BlockSpec((bc, n_t, d), lambda i: (i, 0, 0)),
            pl.BlockSpec((CTX, bc, d), lambda i: (0, i, 0)),
            _const_spec((1, d)),
            _const_spec((d, 2 * d)),
            _const_spec((1, 2 * d)),
            _const_spec(wdw.shape),
            _const_spec((1, d)),
            _const_spec((1, d)),
            _const_spec((1, d)),
            _const_spec((d, d)),
            _const_spec((1, d)),
        ],
        out_specs=[
            pl.BlockSpec((bc, n_t, d), lambda i: (i, 0, 0)),
            pl.BlockSpec((CTX, bc, d), lambda i: (0, i, 0)),
        ],
        out_shape=[
            jax.ShapeDtypeStruct((nb, n_t, d), F32),
            jax.ShapeDtypeStruct((CTX, nb, d), F32),
        ],
        scratch_shapes=[
            pltpu.VMEM((n_t, bc, d), F32),
            pltpu.VMEM((n_t, bc, d), F32),
        ],
        compiler_params=_params("arbitrary"),
        name="conv_mixer_sample",
    )(x, st_t, ng, w1, b1, wdw, bdw, lg, lb, w2, b2)


def _ffn_kernel(xp_ref, xs_ref, ng_ref, wg_ref, wu_ref, wd_ref, fg_ref, *rest,
                final_norm, row_groups, n_prompt_tiles, n_cast, cast_every):
    cast_src = rest[:n_cast]
    op_ref, os_ref = rest[n_cast:n_cast + 2]
    cast_dst = rest[n_cast + 2:]
    _cast_slabs(cast_src, cast_dst, cast_every)

    def ffn(x):
        h = _rmsnorm(x, ng_ref[...]).astype(BF16)
        acts = []
        for k in range(wg_ref.shape[1] // MXU_COLS):
            cols = slice(k * MXU_COLS, (k + 1) * MXU_COLS)
            acts.append((_silu(_dot(h, wg_ref[:, cols])) * _dot(h, wu_ref[:, cols])).astype(BF16))
        y = x + _dot(jnp.concatenate(acts, axis=1), wd_ref[...])
        if final_norm:
            y = _rmsnorm(y, fg_ref[...])
        return y

    i = pl.program_id(0)

    @pl.when(i < n_prompt_tiles)
    def _():
        for rows in _row_groups(xp_ref.shape[0], row_groups):
            op_ref[rows, :] = ffn(xp_ref[rows, :])

    @pl.when(i == n_prompt_tiles)
    def _():
        _store_time_major(os_ref, ffn(_time_major(xs_ref)))


def _ffn(xp, xs, ng, wg, wu, wd, fg, cast_items, *, tm, row_groups, final_norm):
    n, d = xp.shape
    dff = wg.shape[1]
    n_tiles = n // tm
    cast_every = n_tiles // CAST_SLABS
    cast_in, cast_out, cast_shapes = _cast_specs(cast_items, cast_every)
    prompt_spec = pl.BlockSpec((tm, d), lambda i: (jnp.minimum(i, n_tiles - 1), 0))
    outs = pl.pallas_call(
        functools.partial(_ffn_kernel, final_norm=final_norm, row_groups=row_groups, n_prompt_tiles=n_tiles,
                          n_cast=len(cast_items), cast_every=cast_every),
        grid=(n_tiles + 1,),
        in_specs=[
            prompt_spec,
            _const_spec(xs.shape),
            _const_spec((1, d)),
            _const_spec((d, dff)),
            _const_spec((d, dff)),
            _const_spec((dff, d)),
            _const_spec((1, d)),
        ] + cast_in,
        out_specs=[prompt_spec, pl.BlockSpec(xs.shape, lambda i: (0, 0, 0))] + cast_out,
        out_shape=[jax.ShapeDtypeStruct((n, d), F32), jax.ShapeDtypeStruct(xs.shape, F32)] + cast_shapes,
        compiler_params=_params("arbitrary"),
        name="swiglu_ffn_final" if final_norm else "swiglu_ffn",
    )(xp, xs, ng, wg, wu, wd, fg, *[item[0] for item in cast_items])
    return outs[0], outs[1], outs[2:]


def _sgu_prompt_tile(x_ref, o_ref, ng_ref, win_ref, bin_ref, lg_ref, lb_ref, ws_ref, bs_ref, wout_ref, bout_ref, tl):
    c = SGU_CHUNK
    row = lax.broadcasted_iota(jnp.int32, (c, c), 0)
    col = lax.broadcasted_iota(jnp.int32, (c, c), 1)
    causal = (col <= row).astype(F32)
    w_heads = [(ws_ref[hd] * causal).astype(BF16) for hd in range(SGU_HEADS)]

    n_c = tl // c
    x = x_ref[...]
    h = _rmsnorm(x, ng_ref[...]).astype(BF16)

    def in_proj(cols):
        return _gelu_exact(_dot(h, win_ref[:, cols]) + bin_ref[:, cols])

    col_blocks = [slice(k * MXU_COLS, (k + 1) * MXU_COLS) for k in range(2 * D_MODEL // MXU_COLS)]
    half = len(col_blocks) // 2
    v_raw = jnp.concatenate([in_proj(cols) for cols in col_blocks[half:]], axis=1)
    v = _layernorm(v_raw, lg_ref[...], lb_ref[...]).astype(BF16)
    u = jnp.concatenate([in_proj(cols) for cols in col_blocks[:half]], axis=1)

    gated_cols = []
    for hd in range(SGU_HEADS):
        lanes = slice(hd * SGU_HEAD_DIM, (hd + 1) * SGU_HEAD_DIM)
        v_h = jnp.concatenate([v[i * c:(i + 1) * c, lanes] for i in range(n_c)], axis=1)
        m_h = _dot(w_heads[hd], v_h)
        bias_h = bs_ref[:, lanes]
        gated_cols.append(jnp.concatenate(
            [u[i * c:(i + 1) * c, lanes] * (m_h[:, i * SGU_HEAD_DIM:(i + 1) * SGU_HEAD_DIM] + bias_h)
             for i in range(n_c)], axis=0))
    gated = jnp.concatenate(gated_cols, axis=1).astype(BF16)
    for cols in col_blocks[:half]:
        o_ref[:, cols] = x[:, cols] + _dot(gated, wout_ref[:, cols]) + bout_ref[:, cols]


def _sgu_sample_rows(x_ref, o_ref, v_ref, ng_ref, win_ref, bin_ref, lg_ref, lb_ref, w4_ref, b4_ref, wout_ref, bout_ref):
    nb, n_t, _ = x_ref.shape
    x = _time_major(x_ref)
    h = _rmsnorm(x, ng_ref[...]).astype(BF16)
    z = _gelu_exact(_dot(h, win_ref[...]) + bin_ref[...])
    u = z[:, :D_MODEL]
    v = _layernorm(z[:, D_MODEL:], lg_ref[...], lb_ref[...])
    _store_time_major(v_ref, v)
    mixed = []
    for t in range(n_t):
        m_t = jnp.broadcast_to(b4_ref[pl.ds(t, 1), :], (nb, D_MODEL))
        for s in range(t + 1):
            m_t = m_t + w4_ref[t, pl.ds(s, 1), :] * v[s * nb:(s + 1) * nb, :]
        mixed.append(m_t)
    gated = (u * jnp.concatenate(mixed, axis=0)).astype(BF16)
    _store_time_major(o_ref, x + _dot(gated, wout_ref[...]) + bout_ref[...])


def _sgu_kernel(xp_ref, xs_ref, ng_ref, win_ref, bin_ref, lg_ref, lb_ref, ws_ref, bs_ref, w4_ref, b4_ref,
                wout_ref, bout_ref, *rest, tl, n_prompt_tiles, n_cast, cast_every):
    cast_src = rest[:n_cast]
    op_ref, os_ref, vs_ref = rest[n_cast:n_cast + 3]
    cast_dst = rest[n_cast + 3:]
    _cast_slabs(cast_src, cast_dst, cast_every)
    i = pl.program_id(0)

    @pl.when(i < n_prompt_tiles)
    def _():
        _sgu_prompt_tile(xp_ref, op_ref, ng_ref, win_ref, bin_ref, lg_ref, lb_ref, ws_ref, bs_ref,
                         wout_ref, bout_ref, tl)

    @pl.when(i == n_prompt_tiles)
    def _():
        _sgu_sample_rows(xs_ref, os_ref, vs_ref, ng_ref, win_ref, bin_ref, lg_ref, lb_ref, w4_ref, b4_ref,
                         wout_ref, bout_ref)


def _sgu(xp, xs, ng, win, b_in, lg, lb, ws, bs_full, w4, b4, wout, bout, cast_items, *, tl):
    n, d = xp.shape
    n_tiles = n // tl
    cast_every = n_tiles // CAST_SLABS
    cast_in, cast_out, cast_shapes = _cast_specs(cast_items, cast_every)
    prompt_spec = pl.BlockSpec((tl, d), lambda i: (jnp.minimum(i, n_tiles - 1), 0))
    sample_out = pl.BlockSpec(xs.shape, lambda i: (0, 0, 0))
    outs = pl.pallas_call(
        functools.partial(_sgu_kernel, tl=tl, n_prompt_tiles=n_tiles, n_cast=len(cast_items),
                          cast_every=cast_every),
        grid=(n_tiles + 1,),
        in_specs=[
            prompt_spec,
            _const_spec(xs.shape),
            _const_spec((1, d)),
            _const_spec((d, 2 * d)),
            _const_spec((1, 2 * d)),
            _const_spec((1, d)),
            _const_spec((1, d)),
            _const_spec(ws.shape),
            _const_spec(bs_full.shape),
            _const_spec(w4.shape),
            _const_spec(b4.shape),
            _const_spec((d, d)),
            _const_spec((1, d)),
        ] + cast_in,
        out_specs=[prompt_spec, sample_out, sample_out] + cast_out,
        out_shape=[jax.ShapeDtypeStruct((n, d), F32), jax.ShapeDtypeStruct(xs.shape, F32),
                   jax.ShapeDtypeStruct(xs.shape, F32)] + cast_shapes,
        compiler_params=_params("arbitrary"),
        name="sgu_mixer",
    )(xp, xs, ng, win, b_in, lg, lb, ws, bs_full, w4, b4, wout, bout, *[item[0] for item in cast_items])
    return outs[0], outs[1], outs[2], outs[3:]


def kernel(x_prompt, x_sample, state_conv, conv_norm_g, conv_w_pw1, conv_b_pw1, conv_w_dw, conv_b_dw, conv_ln_g, conv_ln_b, conv_w_pw2, conv_b_pw2, sgu_norm_g, sgu_w_in, sgu_b_in, sgu_ln_g, sgu_ln_b, sgu_w_s, sgu_b_s, sgu_w_out, sgu_b_out, ffn_norm_g, ffn_w_gate, ffn_w_up, ffn_w_down, final_norm_g):
    batch, seq, d = x_prompt.shape
    n_t = x_sample.shape[1]
    row = lambda a: a.reshape(1, -1)
    ffn_items = lambda i: [(w, i) for w in (ffn_w_gate, ffn_w_up, ffn_w_down)]

    xp = x_prompt.reshape(batch * seq, d)
    st_t = jnp.transpose(state_conv[0], (1, 0, 2))

    w1 = conv_w_pw1[0].astype(BF16)
    w2 = conv_w_pw2[0].astype(BF16)
    wdw = conv_w_dw[0]
    wdw_c = jnp.pad(wdw, ((0, CTX_PAD - CONV_WIDTH), (0, 0))).reshape(CTX_PAD, N_LANE_CHUNKS, LANES)
    wdw_c = jnp.transpose(wdw_c, (1, 0, 2))
    bdw_c = conv_b_dw[0].reshape(N_LANE_CHUNKS, 1, LANES)
    conv_args = (row(conv_norm_g[0]), w1, row(conv_b_pw1[0]))
    conv_tail = (row(conv_ln_g[0]), row(conv_ln_b[0]), w2, row(conv_b_pw2[0]))

    xp, conv_p, ffn0_bf16 = _conv_prompt(xp, *conv_args, wdw_c, bdw_c, *conv_tail, ffn_items(0),
                                         batch=batch, seq=seq, tl=TOKEN_TILE)
    xs, conv_s_t = _conv_sample(x_sample, st_t, *conv_args, wdw, row(conv_b_dw[0]), *conv_tail,
                                bc=SAMPLE_BATCH_TILE)

    fg = row(final_norm_g)
    xp, xs, (win, wout) = _ffn(xp, xs, row(ffn_norm_g[0]), *ffn0_bf16, fg, [(sgu_w_in, 0), (sgu_w_out, 0)],
                               tm=FFN_TOKEN_TILE, row_groups=FFN0_ROW_GROUPS, final_norm=False)

    sgu_args = (row(sgu_norm_g[0]), win, row(sgu_b_in[0]), row(sgu_ln_g[0]), row(sgu_ln_b[0]))
    bs_full = jnp.repeat(jnp.transpose(sgu_b_s[0]), SGU_HEAD_DIM, axis=1)
    w4 = jnp.repeat(jnp.transpose(sgu_w_s[0][:, :n_t, :n_t], (1, 2, 0)), SGU_HEAD_DIM, axis=2)
    b4 = bs_full[:n_t]

    xp, xs, v_s, ffn1_bf16 = _sgu(xp, xs, *sgu_args, sgu_w_s[0], bs_full, w4, b4, wout, row(sgu_b_out[0]),
                                  ffn_items(1), tl=TOKEN_TILE)

    yp, ys, _ = _ffn(xp, xs, row(ffn_norm_g[1]), *ffn1_bf16, fg, [],
                     tm=FFN_TOKEN_TILE, row_groups=FFN1_ROW_GROUPS, final_norm=True)

    new_conv_sample = jnp.transpose(conv_s_t, (1, 0, 2))[None]
    return (yp.reshape(batch, seq, d), ys, conv_p[None], new_conv_sample, v_s[None])
```
